```python
import math
import jax, jax.numpy as jnp
from jax import lax
import numpy as np

D_MODEL = 2048
BATCH = 4
SEQ = 4096
DEPTH = 1

ATT_HEADS = 16
ATT_KV_HEADS = 2
ATT_HEAD_DIM = 64
WINDOW = 128
ATT_BLOCK = WINDOW
ATT_Q_WIDTH = ATT_HEADS * ATT_HEAD_DIM
ATT_KV_WIDTH = ATT_KV_HEADS * ATT_HEAD_DIM
GLA_HEADS = 4
GLA_DK = D_MODEL // 2
GLA_DV = D_MODEL
GLA_HEAD_K = GLA_DK // GLA_HEADS
GLA_HEAD_V = GLA_DV // GLA_HEADS
GLA_GATE_RANK = 16
GLA_GATE_TEMP = 16.0
GLA_CHUNK = 64
N_EXPERTS = 32
TOP_K = 4
D_FF = D_MODEL
SWIGLU_LIMIT = 7.0
SWIGLU_ALPHA = 1.702
MOE_BLOCK = 256
LN_EPS = 1e-5
DN_ALPHA = (2 * DEPTH) ** 0.25
DN_BETA = (8 * DEPTH) ** -0.25
IN_SPLITS = (ATT_Q_WIDTH, ATT_KV_WIDTH, ATT_KV_WIDTH,
             GLA_DK, GLA_DK, GLA_DV, GLA_DV, GLA_GATE_RANK,
             D_MODEL, D_MODEL)
IN_IS_VALUE = (False, False, True, False, False, True, False, False, False, False)
IN_COLS = sum(IN_SPLITS)

kernel_name = "hybrid_swa_gla_moe_deepnorm_adaln"


def _layernorm(x):
    xf = x.astype(jnp.float32)
    mu = jnp.mean(xf, -1, keepdims=True)
    var = jnp.mean(jnp.square(xf - mu), -1, keepdims=True)
    return ((xf - mu) * lax.rsqrt(var + LN_EPS)).astype(x.dtype)


def _layernorm_affine(x, gain, bias):
    xf = x.astype(jnp.float32)
    mu = jnp.mean(xf, -1, keepdims=True)
    var = jnp.mean(jnp.square(xf - mu), -1, keepdims=True)
    y = (xf - mu) * lax.rsqrt(var + LN_EPS) * gain.astype(jnp.float32) + bias.astype(jnp.float32)
    return y.astype(x.dtype)


def _sliding_window_attention(q, k, v, sinks):
    B, T, Hq, Dh = q.shape
    Hkv = k.shape[2]
    G = Hq // Hkv
    nb = T // ATT_BLOCK
    qb = q.reshape(B, nb, ATT_BLOCK, Hkv, G, Dh)

    def band(t):
        tb = t.reshape(B, nb, ATT_BLOCK, Hkv, Dh)
        prev = jnp.pad(tb[:, :-1], ((0, 0), (1, 0), (0, 0), (0, 0), (0, 0)))
        return jnp.concatenate([prev, tb], axis=2)

    kb, vb = band(k), band(v)
    s = jnp.einsum('bnqkgd,bnskd->bnkgqs', qb, kb,
                   preferred_element_type=jnp.float32) * (Dh ** -0.5)
    qi = jnp.arange(ATT_BLOCK)[:, None]
    si = jnp.arange(2 * ATT_BLOCK)[None, :]
    dist = qi + ATT_BLOCK - si
    blk = jnp.arange(nb)[:, None, None]
    valid = (dist >= 0) & (dist < WINDOW)
    valid = valid[None] & ((blk > 0) | (si[None] >= ATT_BLOCK))
    slopes = (2.0 ** (-8.0 * (jnp.arange(Hq, dtype=jnp.float32) + 1.0) / Hq)).reshape(Hkv, G)
    s = s - slopes[None, None, :, :, None, None] * dist.astype(jnp.float32)
    s = jnp.where(valid[None, :, None, None], s, -jnp.inf)
    sink = sinks.astype(jnp.float32).reshape(Hkv, G)[None, None, :, :, None, None]
    m = jnp.maximum(jnp.max(s, -1, keepdims=True), sink)
    p = jnp.exp(s - m)
    p = p / (jnp.sum(p, -1, keepdims=True) + jnp.exp(sink - m))
    o = jnp.einsum('bnkgqs,bnskd->bnqkgd', p.astype(v.dtype), vb)
    return o.reshape(B, T, Hq * Dh)


def _gla(q, k, v, log_a):
    B, T, H, dk = q.shape
    dv = v.shape[-1]
    C = GLA_CHUNK
    N = T // C

    def chunks(t):
        return t.astype(jnp.float32).reshape(B, N, C, H, t.shape[-1]).transpose(1, 0, 3, 2, 4)

    q = chunks(q) * (dk ** -0.5)
    k, v, g = chunks(k), chunks(v), chunks(log_a)
    b = jnp.cumsum(g, axis=3)
    b_last = b[:, :, :, -1:, :]
    q_e = q * jnp.exp(b)
    k_e = k * jnp.exp(-b)
    k_end = k * jnp.exp(b_last - b)
    causal = jnp.tril(jnp.ones((C, C), dtype=bool))
    A = jnp.where(causal, jnp.einsum('nbhid,nbhjd->nbhij', q_e, k_e), 0.0)
    o_intra = jnp.einsum('nbhij,nbhjv->nbhiv', A, v)

    def step(S, inp):
        qe, ke, vc, dl = inp
        o = jnp.einsum('bhid,bhdv->bhiv', qe, S)
        S = jnp.exp(dl)[..., None] * S + jnp.einsum('bhjd,bhjv->bhdv', ke, vc)
        return S, o

    S0 = jnp.zeros((B, H, dk, dv), jnp.float32)
    _, o_inter = lax.scan(step, S0, (q_e, k_end, v, b_last[:, :, :, 0, :]))
    o = o_intra + o_inter
    return o.transpose(1, 0, 3, 2, 4).reshape(B, T, H, dv)


def _head_rmsnorm(o, gain):
    return o * lax.rsqrt(jnp.mean(jnp.square(o), -1, keepdims=True) + LN_EPS) * gain.astype(jnp.float32)


def _moe(h, w_router, b_router, w_gate_up, b_gate_up, w_down, b_down):
    B, T, D = h.shape
    xt = h.reshape(-1, D)
    n_tok = xt.shape[0]
    logits = (xt @ w_router).astype(jnp.float32) + b_router.astype(jnp.float32)
    top_val, top_idx = lax.top_k(logits, TOP_K)
    top_w = jax.nn.softmax(top_val, axis=-1)
    n_rows = n_tok * TOP_K
    e_flat = top_idx.reshape(-1)
    w_flat = top_w.reshape(-1)
    tok_flat = jnp.arange(n_rows) // TOP_K
    order = jnp.argsort(e_flat)
    e_s, tok_s, w_s = e_flat[order], tok_flat[order], w_flat[order]
    counts = jnp.bincount(e_flat, length=N_EXPERTS)
    starts = jnp.cumsum(counts) - counts
    padded = (counts + MOE_BLOCK - 1) // MOE_BLOCK * MOE_BLOCK
    pad_end = jnp.cumsum(padded)
    pad_start = pad_end - padded
    dest = pad_start[e_s] + jnp.arange(n_rows) - starts[e_s]
    cap = -(-n_rows // MOE_BLOCK) * MOE_BLOCK + N_EXPERTS * MOE_BLOCK
    n_blocks = cap // MOE_BLOCK
    xs = jnp.zeros((cap, D), h.dtype).at[dest].set(xt[tok_s])
    blk_e = jnp.minimum(jnp.searchsorted(pad_end, jnp.arange(n_blocks) * MOE_BLOCK, side='right'),
                        N_EXPERTS - 1)

    def expert_block(args):
        xb, e = args
        gu = xb @ w_gate_up[e] + b_gate_up[e]
        gate = jnp.minimum(gu[:, 0::2], SWIGLU_LIMIT)
        up = jnp.clip(gu[:, 1::2], -SWIGLU_LIMIT, SWIGLU_LIMIT)
        act = (up + 1.0) * gate * jax.nn.sigmoid(SWIGLU_ALPHA * gate)
        return act @ w_down[e] + b_down[e]

    ys = lax.map(expert_block, (xs.reshape(n_blocks, MOE_BLOCK, D), blk_e))
    y_rows = ys.reshape(cap, D)[dest].astype(jnp.float32) * w_s[:, None]
    out = jax.ops.segment_sum(y_rows, tok_s, num_segments=n_tok)
    return out.astype(h.dtype).reshape(B, T, D)


def setup_inputs(seed: int = 0) -> dict:
    key = jax.random.key(seed)
    ks = iter(jax.random.split(key, 48))

    def nrm(shape, scale):
        return jax.random.normal(next(ks), shape, jnp.float32) * scale

    L, D, E, F = DEPTH, D_MODEL, N_EXPERTS, D_FF
    x = nrm((BATCH, SEQ, D), 1.0)
    c = nrm((BATCH, D), 1.0)
    w_ada = nrm((L, D, 6 * D), 0.5 * D ** -0.5)
    b_ada = nrm((L, 6 * D), 0.02)
    w_in = jnp.concatenate(
        [nrm((L, D, w), D ** -0.5 * (DN_BETA if is_v else 1.0))
         for w, is_v in zip(IN_SPLITS, IN_IS_VALUE)], axis=-1)
    w_gla_gate_up = nrm((L, GLA_GATE_RANK, GLA_DK), GLA_GATE_RANK ** -0.5)
    b_gla_gate = nrm((L, GLA_DK), 0.1)
    attn_sinks = nrm((L, ATT_HEADS), 1.0)
    gla_norm_gain = 1.0 + nrm((L, GLA_DV), 0.02)
    w_branch_att = nrm((L, ATT_Q_WIDTH, D), ATT_Q_WIDTH ** -0.5)
    w_branch_gla = nrm((L, GLA_DV, D), GLA_DV ** -0.5)
    w_out = nrm((L, D, D), D ** -0.5 * DN_BETA)
    ln1_gain = 1.0 + nrm((L, D), 0.02)
    ln1_bias = nrm((L, D), 0.02)
    w_router = nrm((L, D, E), D ** -0.5)
    b_router = nrm((L, E), 0.01)
    w_gate_up = nrm((L, E, D, 2 * F), D ** -0.5)
    b_gate_up = nrm((L, E, 2 * F), 0.02)
    w_down = nrm((L, E, F, D), F ** -0.5 * DN_BETA)
    b_down = nrm((L, E, D), 0.02)
    ln2_gain = 1.0 + nrm((L, D), 0.02)
    ln2_bias = nrm((L, D), 0.02)
    return {"x": x, "c": c, "w_ada": w_ada, "b_ada": b_ada, "w_in": w_in,
            "w_gla_gate_up": w_gla_gate_up, "b_gla_gate": b_gla_gate,
            "attn_sinks": attn_sinks, "gla_norm_gain": gla_norm_gain,
            "w_branch_att": w_branch_att, "w_branch_gla": w_branch_gla, "w_out": w_out,
            "ln1_gain": ln1_gain, "ln1_bias": ln1_bias,
            "w_router": w_router, "b_router": b_router,
            "w_gate_up": w_gate_up, "b_gate_up": b_gate_up,
            "w_down": w_down, "b_down": b_down,
            "ln2_gain": ln2_gain, "ln2_bias": ln2_bias}


def reference(x, c, w_ada, b_ada, w_in, w_gla_gate_up, b_gla_gate, attn_sinks, gla_norm_gain,
              w_branch_att, w_branch_gla, w_out, ln1_gain, ln1_bias, w_router, b_router,
              w_gate_up, b_gate_up, w_down, b_down, ln2_gain, ln2_bias):
    B, T, D = x.shape
    offs = np.cumsum(IN_SPLITS)[:-1].tolist()
    for l in range(DEPTH):
        ada = (jax.nn.silu(c) @ w_ada[l] + b_ada[l])[:, None, :]
        sh1, sc1, gt1, sh2, sc2, gt2 = jnp.split(ada, 6, axis=-1)

        h = _layernorm(x) * (1.0 + sc1) + sh1
        proj = h @ w_in[l]
        q_a, k_a, v_a, q_g, k_g, v_g, r_g, a_lr, g_a, g_g = jnp.split(proj, offs, axis=-1)
        y_att = _sliding_window_attention(
            q_a.reshape(B, T, ATT_HEADS, ATT_HEAD_DIM),
            k_a.reshape(B, T, ATT_KV_HEADS, ATT_HEAD_DIM),
            v_a.reshape(B, T, ATT_KV_HEADS, ATT_HEAD_DIM),
            attn_sinks[l])
        log_a = jax.nn.log_sigmoid((a_lr @ w_gla_gate_up[l] + b_gla_gate[l]).astype(jnp.float32)) / GLA_GATE_TEMP
        o_g = _gla(q_g.reshape(B, T, GLA_HEADS, GLA_HEAD_K),
                   k_g.reshape(B, T, GLA_HEADS, GLA_HEAD_K),
                   v_g.reshape(B, T, GLA_HEADS, GLA_HEAD_V),
                   log_a.reshape(B, T, GLA_HEADS, GLA_HEAD_K))
        o_g = _head_rmsnorm(o_g, gla_norm_gain[l].reshape(GLA_HEADS, GLA_HEAD_V))
        y_gla = o_g.reshape(B, T, GLA_DV).astype(x.dtype) * jax.nn.silu(r_g)
        merged = (jax.nn.sigmoid(g_a) * (y_att @ w_branch_att[l])
                  + jax.nn.sigmoid(g_g) * (y_gla @ w_branch_gla[l]))
        mix = merged @ w_out[l]
        x = _layernorm_affine(DN_ALPHA * x + gt1 * mix, ln1_gain[l], ln1_bias[l])

        h = _layernorm(x) * (1.0 + sc2) + sh2
        ffn = _moe(h, w_router[l], b_router[l], w_gate_up[l], b_gate_up[l], w_down[l], b_down[l])
        x = _layernorm_affine(DN_ALPHA * x + gt2 * ffn, ln2_gain[l], ln2_bias[l])
    return x
```

```python
import jax
import jax.numpy as jnp
from jax import lax
from jax.experimental import pallas as pl
from jax.experimental.pallas import tpu as pltpu

F32 = jnp.float32
BF16 = jnp.bfloat16
I32 = jnp.int32

ATT_HEADS = 16
ATT_KV_HEADS = 2
ATT_HEAD_DIM = 64
WINDOW = 128
GLA_HEADS = 4
GLA_GATE_RANK = 16
GLA_GATE_TEMP = 16.0
GLA_CHUNK = 64
N_EXPERTS = 32
TOP_K = 4
SWIGLU_LIMIT = 7.0
SWIGLU_ALPHA = 1.702
LN_EPS = 1e-5
DEPTH = 1
DN_ALPHA = (2 * DEPTH) ** 0.25

LANES = 128
VMEM_LIMIT = 56 * 1024 * 1024

MOE_BLOCK = 256
PROJ_TM = 1024
PROJ_TN = 1152
GLA_STEP = 256
MERGE_TM = 256
ROW_TM = 256
EXPERT_TF = 512
LN_ROWS = 256

NT_DIMS = (((1,), (1,)), ((), ()))
TN_DIMS = (((0,), (0,)), ((), ()))


def _cparams(sem, vmem=VMEM_LIMIT):
    return pltpu.CompilerParams(dimension_semantics=sem, vmem_limit_bytes=vmem)


def _ln(x):
    mu = jnp.mean(x, axis=-1, keepdims=True)
    xc = x - mu
    var = jnp.mean(xc * xc, axis=-1, keepdims=True)
    return xc * lax.rsqrt(var + LN_EPS)


def _sigmoid(x):
    return 1.0 / (1.0 + jnp.exp(-x))


def _ada_kernel(c_ref, w_ref, b_ref, o_ref):
    c = c_ref[...]
    s = c * _sigmoid(c)
    o_ref[...] = jnp.dot(s.astype(BF16), w_ref[...].astype(BF16),
                         preferred_element_type=F32) + b_ref[...]


def _ada(c, w_ada, b_ada):
    bsz, d = c.shape
    n = w_ada.shape[1]
    tn = 1536
    return pl.pallas_call(
        _ada_kernel,
        out_shape=jax.ShapeDtypeStruct((bsz, n), F32),
        grid=(n // tn,),
        in_specs=[pl.BlockSpec((bsz, d), lambda j: (0, 0)),
                  pl.BlockSpec((d, tn), lambda j: (0, j)),
                  pl.BlockSpec((1, tn), lambda j: (0, j))],
        out_specs=pl.BlockSpec((bsz, tn), lambda j: (0, j)),
        compiler_params=_cparams(("arbitrary",)),
        name="ada",
    )(c, w_ada, b_ada.reshape(1, n))


def _proj_kernel(x_ref, sc_ref, sh_ref, w_ref, walr_ref, o_ref, alr_ref, h_scr):
    @pl.when(pl.program_id(1) == 0)
    def _():
        tm = x_ref.shape[0]
        sub = min(LN_ROWS, tm)
        for r0 in range(0, tm, sub):
            rows = slice(r0, r0 + sub)
            h = _ln(x_ref[rows, :]) * (1.0 + sc_ref[...]) + sh_ref[...]
            h_scr[rows, :] = h.astype(BF16)
        alr_ref[...] = jnp.dot(h_scr[...], walr_ref[...], preferred_element_type=F32)

    o_ref[...] = jnp.dot(h_scr[...], w_ref[...],
                         preferred_element_type=F32).astype(BF16)


def _proj(x2, ada3, w_main, w_alr, seq):
    n, d = x2.shape
    cols = w_main.shape[1]
    tm, tn = min(PROJ_TM, seq), PROJ_TN
    per_b = seq // tm
    return pl.pallas_call(
        _proj_kernel,
        out_shape=(jax.ShapeDtypeStruct((n, cols), BF16),
                   jax.ShapeDtypeStruct((n, LANES), F32)),
        grid=(n // tm, cols // tn),
        in_specs=[pl.BlockSpec((tm, d), lambda i, j: (i, 0)),
                  pl.BlockSpec((None, 1, d), lambda i, j: ((i // per_b) * 6 + 1, 0, 0)),
                  pl.BlockSpec((None, 1, d), lambda i, j: ((i // per_b) * 6 + 0, 0, 0)),
                  pl.BlockSpec((d, tn), lambda i, j: (0, j)),
                  pl.BlockSpec((d, LANES), lambda i, j: (0, 0))],
        out_specs=(pl.BlockSpec((tm, tn), lambda i, j: (i, j)),
                   pl.BlockSpec((tm, LANES), lambda i, j: (i, 0))),
        scratch_shapes=[pltpu.VMEM((tm, d), BF16)],
        compiler_params=_cparams(("arbitrary", "arbitrary")),
        name="proj",
    )(x2, ada3, ada3, w_main, w_alr)


def _swa_kernel(sink_ref, q_ref, kp_ref, kc_ref, vp_ref, vc_ref, o_ref):
    n = pl.program_id(1)
    blk = WINDOW
    half = ATT_HEAD_DIM
    group = ATT_HEADS // ATT_KV_HEADS

    lane = lax.broadcasted_iota(I32, (2 * blk, 2 * half), 1)
    lo = lane < half

    def variants(prev_ref, cur_ref, scale):
        band = jnp.concatenate([prev_ref[...], cur_ref[...]], axis=0).astype(F32) * scale
        rolled = pltpu.roll(band, half, axis=1)
        zero = jnp.zeros_like(band)
        return ((jnp.where(lo, band, zero).astype(BF16), jnp.where(lo, zero, rolled).astype(BF16)),
                (jnp.where(lo, rolled, zero).astype(BF16), jnp.where(lo, zero, band).astype(BF16)))

    kvar = variants(kp_ref, kc_ref, ATT_HEAD_DIM ** -0.5)
    vvar = variants(vp_ref, vc_ref, 1.0)

    qi = lax.broadcasted_iota(I32, (blk, 2 * blk), 0)
    si = lax.broadcasted_iota(I32, (blk, 2 * blk), 1)
    dist = qi + blk - si
    first_key = jnp.where(n > 0, 0, blk)
    valid = (dist >= 0) & (dist < WINDOW) & (si >= first_key)
    dist_f = dist.astype(F32)

    for pair in range(ATT_HEADS // 2):
        q_pair = q_ref[:, pair * 2 * half:(pair + 1) * 2 * half]
        acc = jnp.zeros((blk, 2 * half), F32)
        for sub in range(2):
            h = pair * 2 + sub
            kh = h // group
            slope = 2.0 ** (-8.0 * (h + 1.0) / ATT_HEADS)
            s = lax.dot_general(q_pair, kvar[kh][sub], NT_DIMS, preferred_element_type=F32)
            s = jnp.where(valid, s - slope * dist_f, -jnp.inf)
            sink = sink_ref[h]
            m = jnp.maximum(jnp.max(s, axis=-1, keepdims=True), sink)
            p = jnp.exp(s - m)
            denom = jnp.sum(p, axis=-1, keepdims=True) + jnp.exp(sink - m)
            p = p / denom
            acc = acc + jnp.dot(p.astype(BF16), vvar[kh][sub], preferred_element_type=F32)
        o_ref[:, pair * 2 * half:(pair + 1) * 2 * half] = acc.astype(BF16)


def _swa(proj, sinks, bsz, seq, q_blk, k_blk, v_blk):
    n = bsz * seq
    nb = seq // WINDOW
    qw = ATT_HEADS * ATT_HEAD_DIM
    kw = ATT_KV_HEADS * ATT_HEAD_DIM
    cur = lambda col: (lambda b, i: (b * nb + i, col))
    prev = lambda col: (lambda b, i: (b * nb + jnp.maximum(i - 1, 0), col))
    return pl.pallas_call(
        _swa_kernel,
        out_shape=jax.ShapeDtypeStruct((n, qw), BF16),
        grid=(bsz, nb),
        in_specs=[pl.BlockSpec(memory_space=pltpu.SMEM),
                  pl.BlockSpec((WINDOW, qw), cur(q_blk)),
                  pl.BlockSpec((WINDOW, kw), prev(k_blk)),
                  pl.BlockSpec((WINDOW, kw), cur(k_blk)),
                  pl.BlockSpec((WINDOW, kw), prev(v_blk)),
                  pl.BlockSpec((WINDOW, kw), cur(v_blk))],
        out_specs=pl.BlockSpec((WINDOW, qw), lambda b, i: (b * nb + i, 0)),
        compiler_params=_cparams(("arbitrary", "arbitrary")),
        name="swa",
    )(sinks, proj, proj, proj, proj, proj)


def _gla_kernel(q_ref, k_ref, v_ref, r_ref, alr_ref, wg_ref, bg_ref, gain_ref, o_ref, st_scr):
    c = GLA_CHUNK
    ts, dk = q_ref.shape
    nc = ts // c

    @pl.when(pl.program_id(2) == 0)
    def _():
        st_scr[...] = jnp.zeros_like(st_scr)

    z = jnp.dot(alr_ref[...], wg_ref[...], preferred_element_type=F32,
                precision=lax.Precision.HIGHEST) + bg_ref[...]
    log_a = (jnp.minimum(z, 0.0) - jnp.log1p(jnp.exp(-jnp.abs(z)))) / GLA_GATE_TEMP
    row = lax.broadcasted_iota(I32, (ts, ts), 0)
    col = lax.broadcasted_iota(I32, (ts, ts), 1)
    causal = (row >= col) & ((row // c) == (col // c))
    b = jnp.dot(causal.astype(F32), log_a, preferred_element_type=F32,
                precision=lax.Precision.HIGHEST)
    b_last = [b[(ci + 1) * c - 1:(ci + 1) * c, :] for ci in range(nc)]
    b_end = jnp.concatenate([jnp.broadcast_to(bl, (c, dk)) for bl in b_last], axis=0)
    q = q_ref[...].astype(F32)
    k = k_ref[...].astype(F32)
    v = v_ref[...]
    q_e = (q * jnp.exp(b) * (dk ** -0.5)).astype(BF16)
    k_e = (k * jnp.exp(-b)).astype(BF16)
    k_end = (k * jnp.exp(b_end - b)).astype(BF16)
    a = lax.dot_general(q_e, k_e, NT_DIMS, preferred_element_type=F32)
    a = jnp.where(causal, a, 0.0).astype(BF16)
    o_intra = jnp.dot(a, v, preferred_element_type=F32)

    st = st_scr[...]
    o_inter = []
    for ci in range(nc):
        rows = slice(ci * c, (ci + 1) * c)
        o_inter.append(lax.dot_general(q_e[rows, :], st.astype(BF16), NT_DIMS,
                                       preferred_element_type=F32))
        upd = lax.dot_general(v[rows, :], k_end[rows, :], TN_DIMS, preferred_element_type=F32)
        st = st * jnp.exp(b_last[ci]) + upd
    st_scr[...] = st

    o = o_intra + jnp.concatenate(o_inter, axis=0)
    o = o * lax.rsqrt(jnp.mean(o * o, axis=-1, keepdims=True) + LN_EPS) * gain_ref[...]
    r = r_ref[...].astype(F32)
    o_ref[...] = (o * (r * _sigmoid(r))).astype(BF16)


def _gla(proj, alr, wg_pad, bg, gain, bsz, seq, q_blk, k_blk, v_blk, r_blk):
    n = bsz * seq
    dk = wg_pad.shape[1] // GLA_HEADS
    dv = gain.shape[1] // GLA_HEADS
    ts = min(GLA_STEP, seq)
    steps = seq // ts
    rowmap = lambda col0: (lambda b, h, t: (b * steps + t, col0 + h))
    return pl.pallas_call(
        _gla_kernel,
        out_shape=jax.ShapeDtypeStruct((n, GLA_HEADS * dv), BF16),
        grid=(bsz, GLA_HEADS, steps),
        in_specs=[pl.BlockSpec((ts, dk), rowmap(q_blk)),
                  pl.BlockSpec((ts, dk), rowmap(k_blk)),
                  pl.BlockSpec((ts, dv), rowmap(v_blk)),
                  pl.BlockSpec((ts, dv), rowmap(r_blk)),
                  pl.BlockSpec((ts, LANES), lambda b, h, t: (b * steps + t, 0)),
                  pl.BlockSpec((LANES, dk), lambda b, h, t: (0, h)),
                  pl.BlockSpec((1, dk), lambda b, h, t: (0, h)),
                  pl.BlockSpec((1, dv), lambda b, h, t: (0, h))],
        out_specs=pl.BlockSpec((ts, dv), rowmap(0)),
        scratch_shapes=[pltpu.VMEM((dv, dk), F32)],
        compiler_params=_cparams(("arbitrary", "arbitrary", "arbitrary")),
        name="gla",
    )(proj, proj, proj, proj, alr, wg_pad, bg, gain)


def _merge_kernel(ya_ref, yg_ref, ga_ref, gg_ref, x_ref, gt1_ref, sc2_ref, sh2_ref,
                  pa_ref, pg_ref, wo_ref, g1_ref, b1_ref, wrh_ref, wrl_ref, br_ref,
                  x1_ref, h2_ref, lt_ref):
    a = jnp.dot(ya_ref[...], pa_ref[...], preferred_element_type=F32)
    g = jnp.dot(yg_ref[...], pg_ref[...], preferred_element_type=F32)
    merged = (_sigmoid(ga_ref[...].astype(F32)) * a + _sigmoid(gg_ref[...].astype(F32)) * g)
    mix = jnp.dot(merged.astype(BF16), wo_ref[...], preferred_element_type=F32)
    x1 = _ln(DN_ALPHA * x_ref[...] + gt1_ref[...] * mix) * g1_ref[...] + b1_ref[...]
    x1_ref[...] = x1
    h2 = _ln(x1) * (1.0 + sc2_ref[...]) + sh2_ref[...]
    h2_ref[...] = h2
    h2_hi = h2.astype(BF16)
    h2_lo = (h2 - h2_hi.astype(F32)).astype(BF16)
    wrh = wrh_ref[...]
    logits = (lax.dot_general(wrh, h2_hi, NT_DIMS, preferred_element_type=F32)
              + lax.dot_general(wrh, h2_lo, NT_DIMS, preferred_element_type=F32)
              + lax.dot_general(wrl_ref[...], h2_hi, NT_DIMS, preferred_element_type=F32))
    lt_ref[...] = logits + br_ref[...]


def _merge(y_att, y_gla, proj, x2, ada3, p_a, p_g, w_o, g1, b1, wr_hi, wr_lo, b_r, seq,
           ga_blk, gg_blk):
    n, d = x2.shape
    tm = min(MERGE_TM, seq)
    per_b = seq // tm
    ne = wr_hi.shape[0]
    const = lambda shape: pl.BlockSpec(shape, lambda i: (0,) * len(shape),
                                       pipeline_mode=pl.Buffered(1))
    adarow = lambda k: pl.BlockSpec((None, 1, d), lambda i: ((i // per_b) * 6 + k, 0, 0))
    return pl.pallas_call(
        _merge_kernel,
        out_shape=(jax.ShapeDtypeStruct((n, d), F32),
                   jax.ShapeDtypeStruct((n, d), F32),
                   jax.ShapeDtypeStruct((ne, n), F32)),
        grid=(n // tm,),
        in_specs=[pl.BlockSpec((tm, y_att.shape[1]), lambda i: (i, 0)),
                  pl.BlockSpec((tm, d), lambda i: (i, 0)),
                  pl.BlockSpec((tm, d), lambda i: (i, ga_blk)),
                  pl.BlockSpec((tm, d), lambda i: (i, gg_blk)),
                  pl.BlockSpec((tm, d), lambda i: (i, 0)),
                  adarow(2), adarow(4), adarow(3),
                  const(p_a.shape), const(p_g.shape), const(w_o.shape),
                  const((1, d)), const((1, d)),
                  const(wr_hi.shape), const(wr_lo.shape), const((ne, 1))],
        out_specs=(pl.BlockSpec((tm, d), lambda i: (i, 0)),
                   pl.BlockSpec((tm, d), lambda i: (i, 0)),
                   pl.BlockSpec((ne, tm), lambda i: (0, i))),
        compiler_params=_cparams(("arbitrary",)),
        name="merge",
    )(y_att, y_gla, proj, proj, x2, ada3, ada3, ada3, p_a, p_g, w_o, g1, b1, wr_hi, wr_lo, b_r)


def _route_kernel(lt_ref, dest_ref, w_ref, blke_ref, bend_ref, oh_scr, rank_scr):
    ne, n = lt_ref.shape
    logits = lt_ref[...]
    eidx = lax.broadcasted_iota(I32, (ne, n), 0).astype(F32)
    vals, idxs = [], []
    for _ in range(TOP_K):
        m = jnp.max(logits, axis=0, keepdims=True)
        idx = jnp.min(jnp.where(logits == m, eidx, float(ne)), axis=0, keepdims=True)
        vals.append(m)
        idxs.append(idx)
        logits = jnp.where(eidx == idx, -jnp.inf, logits)
    exps = [jnp.exp(v - vals[0]) for v in vals]
    total = exps[0] + exps[1] + exps[2] + exps[3]
    for k in range(TOP_K):
        w_ref[k:k + 1, :] = exps[k] / total

    onehot = jnp.zeros((ne, n), F32)
    for k in range(TOP_K):
        onehot = onehot + (eidx == idxs[k]).astype(F32)
    oh_scr[...] = onehot.astype(BF16)

    r = lax.broadcasted_iota(I32, (LANES, 2 * LANES), 0)
    cidx = lax.broadcasted_iota(I32, (LANES, 2 * LANES), 1)
    scan_mat = ((r < cidx) | (cidx >= LANES)).astype(BF16)
    carry = jnp.zeros((ne, LANES), F32)
    for t in range(n // LANES):
        cols = slice(t * LANES, (t + 1) * LANES)
        both = jnp.dot(oh_scr[:, cols], scan_mat, preferred_element_type=F32)
        rank_scr[:, cols] = both[:, :LANES] + carry
        carry = carry + both[:, LANES:]

    counts = carry
    nblk = jnp.floor((counts + (MOE_BLOCK - 1)) * (1.0 / MOE_BLOCK))
    er = lax.broadcasted_iota(I32, (ne, ne), 0)
    ec = lax.broadcasted_iota(I32, (ne, ne), 1)
    strict_lower = (ec < er).astype(BF16)
    blk_start = jnp.dot(strict_lower, nblk.astype(BF16), preferred_element_type=F32)
    blk_end = blk_start + nblk
    row_start = blk_start[:, :1] * float(MOE_BLOCK)

    pos = rank_scr[...] + row_start
    for k in range(TOP_K):
        d = jnp.sum(jnp.where(eidx == idxs[k], pos, 0.0), axis=0, keepdims=True)
        dest_ref[k:k + 1, :] = d.astype(I32)

    nb_pad = blke_ref.shape[1]
    j = lax.broadcasted_iota(I32, (ne, nb_pad), 1).astype(F32)
    blke = jnp.sum((blk_end[:, :1] <= j).astype(F32), axis=0, keepdims=True)
    blke_ref[...] = jnp.minimum(blke, float(ne - 1)).astype(I32)
    bend_ref[...] = blk_end.astype(I32)


def _route(logits_t, n_blocks):
    ne, n = logits_t.shape
    nb_pad = -(-n_blocks // LANES) * LANES
    return pl.pallas_call(
        _route_kernel,
        out_shape=(jax.ShapeDtypeStruct((TOP_K, n), I32),
                   jax.ShapeDtypeStruct((TOP_K, n), F32),
                   jax.ShapeDtypeStruct((1, nb_pad), I32),
                   jax.ShapeDtypeStruct((ne, LANES), I32)),
        scratch_shapes=[pltpu.VMEM((ne, n), BF16), pltpu.VMEM((ne, n), F32)],
        compiler_params=pltpu.CompilerParams(vmem_limit_bytes=VMEM_LIMIT),
        name="route",
    )(logits_t)


def _dispatch_kernel(dest_ref, bend_ref, h_ref, xs_ref, zero_scr, sem, zsem):
    tm = h_ref.shape[0]
    n = dest_ref.shape[0] // TOP_K
    ne = bend_ref.shape[0]
    base = pl.program_id(0) * tm

    @pl.when(pl.program_id(0) == 0)
    def _():
        zero_scr[...] = jnp.zeros_like(zero_scr)

        def zero_block(blk):
            start = pl.multiple_of(blk * MOE_BLOCK, MOE_BLOCK)
            return pltpu.make_async_copy(zero_scr, xs_ref.at[pl.ds(start, MOE_BLOCK), :], zsem)

        def nonempty(e):
            return bend_ref[e] > (bend_ref[e - 1] if e > 0 else 0)

        for e in range(ne):
            @pl.when(nonempty(e))
            def _():
                zero_block(bend_ref[e] - 1).start()
        for e in range(ne):
            @pl.when(nonempty(e))
            def _():
                zero_block(bend_ref[e] - 1).wait()

        n_blocks = xs_ref.shape[0] // MOE_BLOCK

        def start_tail(blk, carry):
            zero_block(blk).start()
            return carry

        def wait_tail(blk, carry):
            zero_block(blk).wait()
            return carry

        lax.fori_loop(bend_ref[ne - 1], n_blocks, start_tail, 0)
        lax.fori_loop(bend_ref[ne - 1], n_blocks, wait_tail, 0)

    def row_copy(r, d):
        return pltpu.make_async_copy(h_ref.at[pl.ds(r, 1), :], xs_ref.at[pl.ds(d, 1), :], sem)

    def issue(r, carry):
        for k in range(TOP_K):
            row_copy(r, dest_ref[k * n + base + r]).start()
        return carry

    def drain(r, carry):
        for k in range(TOP_K):
            row_copy(r, dest_ref[k * n + base + r]).wait()
        return carry

    lax.fori_loop(0, tm, issue, 0)
    lax.fori_loop(0, tm, drain, 0)


def _dispatch(dest_flat, blk_end, h2, cap):
    n, d = h2.shape
    tm = min(ROW_TM, n)
    return pl.pallas_call(
        _dispatch_kernel,
        out_shape=jax.ShapeDtypeStruct((cap, d), F32),
        grid_spec=pltpu.PrefetchScalarGridSpec(
            num_scalar_prefetch=2,
            grid=(n // tm,),
            in_specs=[pl.BlockSpec((tm, d), lambda i, dest, bend: (i, 0))],
            out_specs=pl.BlockSpec(memory_space=pl.ANY),
            scratch_shapes=[pltpu.VMEM((MOE_BLOCK, d), F32),
                            pltpu.SemaphoreType.DMA(()),
                            pltpu.SemaphoreType.DMA(())]),
        compiler_params=_cparams(("arbitrary",)),
        name="dispatch",
    )(dest_flat, blk_end, h2)


GU_CHUNK = 2 * LANES


def _expert_kernel(blke_ref, bend_ref, xs_ref, wgu_ref, wd_ref, bg_ref, bu_ref, bd_ref,
                   ys_ref, wg_scr, wu_scr, wd_scr, gu_stage, wd_stage, sems):
    j = pl.program_id(0)
    ne = bend_ref.shape[0]
    n_used = bend_ref[ne - 1]
    e = blke_ref[j]
    d, f = wg_scr.shape
    n_chunks = f // LANES
    first = (j == 0) | (e != blke_ref[jnp.maximum(j - 1, 0)])

    @pl.when((j < n_used) & first)
    def _():
        def gu_copy(c, slot):
            return pltpu.make_async_copy(wgu_ref.at[e, :, pl.ds(c * GU_CHUNK, GU_CHUNK)],
                                         gu_stage.at[slot], sems.at[0, slot])

        def wd_copy(c, slot):
            return pltpu.make_async_copy(wd_ref.at[e, pl.ds(c * LANES, LANES), :],
                                         wd_stage.at[slot], sems.at[1, slot])

        src = lax.broadcasted_iota(I32, (GU_CHUNK, GU_CHUNK), 0)
        dst = lax.broadcasted_iota(I32, (GU_CHUNK, GU_CHUNK), 1)
        perm = (dst == (src >> 1) + (src & 1) * LANES).astype(BF16)

        gu_copy(0, 0).start()
        wd_copy(0, 0).start()
        for c in range(n_chunks):
            slot = c % 2
            if c + 1 < n_chunks:
                gu_copy(c + 1, 1 - slot).start()
                wd_copy(c + 1, 1 - slot).start()
            gu_copy(c, slot).wait()
            wd_copy(c, slot).wait()
            cols = slice(c * LANES, (c + 1) * LANES)
            sep = jnp.dot(gu_stage[slot].astype(BF16), perm, preferred_element_type=F32)
            wg_scr[:, cols] = sep[:, :LANES].astype(BF16)
            wu_scr[:, cols] = sep[:, LANES:].astype(BF16)
            wd_scr[cols, :] = wd_stage[slot].astype(BF16)

    @pl.when(j < n_used)
    def _():
        x = xs_ref[...].astype(BF16)
        tf = min(EXPERT_TF, f)
        acc = jnp.zeros(ys_ref.shape, F32)
        for c in range(f // tf):
            cols = slice(c * tf, (c + 1) * tf)
            g = jnp.dot(x, wg_scr[:, cols], preferred_element_type=F32) + bg_ref[:, cols]
            u = jnp.dot(x, wu_scr[:, cols], preferred_element_type=F32) + bu_ref[:, cols]
            gate = jnp.minimum(g, SWIGLU_LIMIT)
            up = jnp.clip(u, -SWIGLU_LIMIT, SWIGLU_LIMIT)
            act = (up + 1.0) * gate * _sigmoid(SWIGLU_ALPHA * gate)
            acc = acc + jnp.dot(act.astype(BF16), wd_scr[cols, :], preferred_element_type=F32)
        ys_ref[...] = acc + bd_ref[...]

    @pl.when(j >= n_used)
    def _():
        ys_ref[...] = jnp.zeros_like(ys_ref)


def _experts(blk_e, blk_end, xs, w_gate_up, w_down, b_gate, b_up, b_down):
    cap, d = xs.shape
    ne, f, _ = w_down.shape
    n_blocks = cap // MOE_BLOCK
    bspec = lambda width: pl.BlockSpec((None, 1, width), lambda j, be, bend: (be[j], 0, 0))
    xs_map = lambda j, be, bend: (jnp.minimum(j, bend[ne - 1] - 1), 0)
    return pl.pallas_call(
        _expert_kernel,
        out_shape=jax.ShapeDtypeStruct((cap, d), F32),
        grid_spec=pltpu.PrefetchScalarGridSpec(
            num_scalar_prefetch=2,
            grid=(n_blocks,),
            in_specs=[pl.BlockSpec((MOE_BLOCK, d), xs_map),
                      pl.BlockSpec(memory_space=pl.ANY),
                      pl.BlockSpec(memory_space=pl.ANY),
                      bspec(f), bspec(f), bspec(d)],
            out_specs=pl.BlockSpec((MOE_BLOCK, d), lambda j, be, bend: (j, 0)),
            scratch_shapes=[pltpu.VMEM((d, f), BF16), pltpu.VMEM((d, f), BF16),
                            pltpu.VMEM((f, d), BF16),
                            pltpu.VMEM((2, d, GU_CHUNK), F32), pltpu.VMEM((2, LANES, d), F32),
                            pltpu.SemaphoreType.DMA((2, 2))]),
        compiler_params=_cparams(("arbitrary",)),
        name="experts",
    )(blk_e, blk_end, xs, w_gate_up, w_down, b_gate, b_up, b_down)


def _combine_kernel(dest_ref, ys_ref, wt_ref, x1_ref, gt2_ref, g2_ref, b2_ref, o_ref, buf, sem):
    tm = x1_ref.shape[0]
    n = dest_ref.shape[0] // TOP_K
    base = pl.program_id(0) * tm

    def row_copy(k, r, d):
        return pltpu.make_async_copy(ys_ref.at[pl.ds(d, 1), :], buf.at[k, pl.ds(r, 1), :], sem)

    def issue(r, carry):
        for k in range(TOP_K):
            row_copy(k, r, dest_ref[k * n + base + r]).start()
        return carry

    def drain(r, carry):
        for k in range(TOP_K):
            row_copy(k, r, dest_ref[k * n + base + r]).wait()
        return carry

    lax.fori_loop(0, tm, issue, 0)
    lax.fori_loop(0, tm, drain, 0)

    wt = wt_ref[...]
    ffn = buf[0] * wt[:, 0:1]
    for k in range(1, TOP_K):
        ffn = ffn + buf[k] * wt[:, k:k + 1]
    z = DN_ALPHA * x1_ref[...] + gt2_ref[...] * ffn
    o_ref[...] = _ln(z) * g2_ref[...] + b2_ref[...]


def _combine(dest_flat, ys, w_t, x1, ada3, g2, b2, seq):
    n, d = x1.shape
    tm = min(ROW_TM, seq)
    per_b = seq // tm
    return pl.pallas_call(
        _combine_kernel,
        out_shape=jax.ShapeDtypeStruct((n, d), F32),
        grid_spec=pltpu.PrefetchScalarGridSpec(
            num_scalar_prefetch=1,
            grid=(n // tm,),
            in_specs=[pl.BlockSpec(memory_space=pl.ANY),
                      pl.BlockSpec((tm, TOP_K), lambda i, dest: (i, 0)),
                      pl.BlockSpec((tm, d), lambda i, dest: (i, 0)),
                      pl.BlockSpec((None, 1, d), lambda i, dest: ((i // per_b) * 6 + 5, 0, 0)),
                      pl.BlockSpec((1, d), lambda i, dest: (0, 0)),
                      pl.BlockSpec((1, d), lambda i, dest: (0, 0))],
            out_specs=pl.BlockSpec((tm, d), lambda i, dest: (i, 0)),
            scratch_shapes=[pltpu.VMEM((TOP_K, tm, d), F32),
                            pltpu.SemaphoreType.DMA(())]),
        compiler_params=_cparams(("arbitrary",)),
        name="combine",
    )(dest_flat, ys, w_t, x1, ada3, g2, b2)


def kernel(x, c, w_ada, b_ada, w_in, w_gla_gate_up, b_gla_gate, attn_sinks, gla_norm_gain,
           w_branch_att, w_branch_gla, w_out, ln1_gain, ln1_bias, w_router, b_router,
           w_gate_up, b_gate_up, w_down, b_down, ln2_gain, ln2_bias):
    bsz, seq, d = x.shape
    n = bsz * seq
    qa_w = ATT_HEADS * ATT_HEAD_DIM
    kv_w = ATT_KV_HEADS * ATT_HEAD_DIM
    gk_w = d // 2
    splits = (qa_w, kv_w, kv_w, gk_w, gk_w, d, d, GLA_GATE_RANK, d, d)
    offs = [0]
    for wdt in splits:
        offs.append(offs[-1] + wdt)
    ne = w_router.shape[-1]
    f = w_down.shape[-2]

    x2 = x.reshape(n, d)
    for l in range(w_in.shape[0]):
        wl = w_in[l]
        part = lambda i: wl[:, offs[i]:offs[i + 1]]
        w_main = jnp.concatenate([part(3), part(4), part(5), part(6), part(8), part(9),
                                  part(0), part(1), part(2)], axis=1).astype(BF16)
        w_alr = jnp.pad(part(7), ((0, 0), (0, LANES - GLA_GATE_RANK))).astype(BF16)
        wg_pad = jnp.pad(w_gla_gate_up[l], ((0, LANES - GLA_GATE_RANK), (0, 0)))
        wr_t = w_router[l].T
        wr_hi = wr_t.astype(BF16)
        wr_lo = (wr_t - wr_hi.astype(F32)).astype(BF16)
        bgu = b_gate_up[l].reshape(ne, 1, f, 2)

        ada = _ada(c, w_ada[l], b_ada[l])
        ada3 = ada.reshape(bsz * 6, 1, d)

        proj, alr = _proj(x2, ada3, w_main, w_alr, seq)
        y_att = _swa(proj, attn_sinks[l], bsz, seq,
                     q_blk=(2 * gk_w + 4 * d) // qa_w,
                     k_blk=(2 * gk_w + 4 * d + qa_w) // kv_w,
                     v_blk=(2 * gk_w + 4 * d + qa_w + kv_w) // kv_w)
        dk = gk_w // GLA_HEADS
        dv = d // GLA_HEADS
        y_gla = _gla(proj, alr, wg_pad, b_gla_gate[l].reshape(1, gk_w),
                     gla_norm_gain[l].reshape(1, d), bsz, seq,
                     q_blk=0, k_blk=gk_w // dk, v_blk=2 * gk_w // dv, r_blk=(2 * gk_w + d) // dv)
        x1, h2, logits_t = _merge(
            y_att, y_gla, proj, x2, ada3,
            w_branch_att[l].astype(BF16), w_branch_gla[l].astype(BF16), w_out[l].astype(BF16),
            ln1_gain[l].reshape(1, d), ln1_bias[l].reshape(1, d),
            wr_hi, wr_lo, b_router[l].reshape(ne, 1), seq,
            ga_blk=(2 * gk_w + 2 * d) // d, gg_blk=(2 * gk_w + 3 * d) // d)

        n_rows = n * TOP_K
        cap = -(-n_rows // MOE_BLOCK) * MOE_BLOCK + ne * MOE_BLOCK
        n_blocks = cap // MOE_BLOCK
        dest, w_top, blk_e, blk_end = _route(logits_t, n_blocks)
        dest_flat = dest.reshape(-1)
        blk_e = blk_e.reshape(-1)[:n_blocks]
        blk_end = blk_end[:, 0]
        xs = _dispatch(dest_flat, blk_end, h2, cap)
        ys = _experts(blk_e, blk_end, xs, w_gate_up[l], w_down[l],
                      bgu[..., 0], bgu[..., 1], b_down[l].reshape(ne, 1, d))
        x2 = _combine(dest_flat, ys, w_top.T, x1, ada3,
                      ln2_gain[l].reshape(1, d), ln2_bias[l].reshape(1, d), seq)
    return x2.reshape(bsz, seq, d)
```

```python
import jax
import jax.numpy as jnp
from jax import lax
from jax.experimental import pallas as pl
from jax.experimental.pallas import tpu as pltpu

F32 = jnp.float32
BF16 = jnp.bfloat16
I32 = jnp.int32

ATT_HEADS = 16
ATT_KV_HEADS = 2
ATT_HEAD_DIM = 64
WINDOW = 128
GLA_HEADS = 4
GLA_GATE_RANK = 16
GLA_GATE_TEMP = 16.0
GLA_CHUNK = 64
N_EXPERTS = 32
TOP_K = 4
SWIGLU_LIMIT = 7.0
SWIGLU_ALPHA = 1.702
LN_EPS = 1e-5
DEPTH = 1
DN_ALPHA = (2 * DEPTH) ** 0.25

LANES = 128
VMEM_LIMIT = 56 * 1024 * 1024

MOE_BLOCK = 256
PROJ_TM = 1024
PROJ_TN = 1152
GLA_STEP = 256
GLA_HEADS_PER_STEP = 4
MERGE_TM = 256
ROW_TM = 256
ROW_UNROLL = 8
EXPERT_TF = 512
LN_ROWS = 256

NT_DIMS = (((1,), (1,)), ((), ()))
TN_DIMS = (((0,), (0,)), ((), ()))


def _cparams(sem, vmem=VMEM_LIMIT):
    return pltpu.CompilerParams(dimension_semantics=sem, vmem_limit_bytes=vmem)


def _ln(x):
    mu = jnp.mean(x, axis=-1, keepdims=True)
    xc = x - mu
    var = jnp.mean(xc * xc, axis=-1, keepdims=True)
    return xc * lax.rsqrt(var + LN_EPS)


def _sigmoid(x):
    return 1.0 / (1.0 + jnp.exp(-x))


def _ada_kernel(c_ref, w_ref, b_ref, o_ref):
    c = c_ref[...]
    s = c * _sigmoid(c)
    o_ref[...] = jnp.dot(s.astype(BF16), w_ref[...].astype(BF16),
                         preferred_element_type=F32) + b_ref[...]


def _ada(c, w_ada, b_ada):
    bsz, d = c.shape
    n = w_ada.shape[1]
    tn = 1536
    return pl.pallas_call(
        _ada_kernel,
        out_shape=jax.ShapeDtypeStruct((bsz, n), F32),
        grid=(n // tn,),
        in_specs=[pl.BlockSpec((bsz, d), lambda j: (0, 0)),
                  pl.BlockSpec((d, tn), lambda j: (0, j)),
                  pl.BlockSpec((1, tn), lambda j: (0, j))],
        out_specs=pl.BlockSpec((bsz, tn), lambda j: (0, j)),
        compiler_params=_cparams(("arbitrary",)),
        name="ada",
    )(c, w_ada, b_ada.reshape(1, n))


def _proj_kernel(x_ref, sc_ref, sh_ref, w_ref, walr_ref, o_ref, alr_ref, h_scr):
    @pl.when(pl.program_id(1) == 0)
    def _():
        tm = x_ref.shape[0]
        sub = min(LN_ROWS, tm)
        for r0 in range(0, tm, sub):
            rows = slice(r0, r0 + sub)
            h = _ln(x_ref[rows, :]) * (1.0 + sc_ref[...]) + sh_ref[...]
            h_scr[rows, :] = h.astype(BF16)
        alr_ref[...] = jnp.dot(h_scr[...], walr_ref[...], preferred_element_type=F32)

    o_ref[...] = jnp.dot(h_scr[...], w_ref[...],
                         preferred_element_type=F32).astype(BF16)


def _proj(x2, ada3, w_main, w_alr, seq):
    n, d = x2.shape
    cols = w_main.shape[1]
    tm, tn = min(PROJ_TM, seq), PROJ_TN
    per_b = seq // tm
    return pl.pallas_call(
        _proj_kernel,
        out_shape=(jax.ShapeDtypeStruct((n, cols), BF16),
                   jax.ShapeDtypeStruct((n, LANES), F32)),
        grid=(n // tm, cols // tn),
        in_specs=[pl.BlockSpec((tm, d), lambda i, j: (i, 0)),
                  pl.BlockSpec((None, 1, d), lambda i, j: ((i // per_b) * 6 + 1, 0, 0)),
                  pl.BlockSpec((None, 1, d), lambda i, j: ((i // per_b) * 6 + 0, 0, 0)),
                  pl.BlockSpec((d, tn), lambda i, j: (0, j)),
                  pl.BlockSpec((d, LANES), lambda i, j: (0, 0))],
        out_specs=(pl.BlockSpec((tm, tn), lambda i, j: (i, j)),
                   pl.BlockSpec((tm, LANES), lambda i, j: (i, 0))),
        scratch_shapes=[pltpu.VMEM((tm, d), BF16)],
        compiler_params=_cparams(("arbitrary", "arbitrary")),
        name="proj",
    )(x2, ada3, ada3, w_main, w_alr)


def _swa_kernel(sink_ref, q_ref, kp_ref, kc_ref, vp_ref, vc_ref, o_ref):
    n = pl.program_id(1)
    blk = WINDOW
    half = ATT_HEAD_DIM
    group = ATT_HEADS // ATT_KV_HEADS

    lane = lax.broadcasted_iota(I32, (2 * blk, 2 * half), 1)
    lo = lane < half

    def variants(prev_ref, cur_ref, scale):
        band = jnp.concatenate([prev_ref[...], cur_ref[...]], axis=0).astype(F32) * scale
        rolled = pltpu.roll(band, half, axis=1)
        zero = jnp.zeros_like(band)
        return ((jnp.where(lo, band, zero).astype(BF16), jnp.where(lo, zero, rolled).astype(BF16)),
                (jnp.where(lo, rolled, zero).astype(BF16), jnp.where(lo, zero, band).astype(BF16)))

    kvar = variants(kp_ref, kc_ref, ATT_HEAD_DIM ** -0.5)
    vvar = variants(vp_ref, vc_ref, 1.0)

    qi = lax.broadcasted_iota(I32, (blk, 2 * blk), 0)
    si = lax.broadcasted_iota(I32, (blk, 2 * blk), 1)
    dist = qi + blk - si
    first_key = jnp.where(n > 0, 0, blk)
    valid = (dist >= 0) & (dist < WINDOW) & (si >= first_key)
    dist_f = dist.astype(F32)

    for pair in range(ATT_HEADS // 2):
        q_pair = q_ref[:, pair * 2 * half:(pair + 1) * 2 * half]
        acc = jnp.zeros((blk, 2 * half), F32)
        for sub in range(2):
            h = pair * 2 + sub
            kh = h // group
            slope = 2.0 ** (-8.0 * (h + 1.0) / ATT_HEADS)
            s = lax.dot_general(q_pair, kvar[kh][sub], NT_DIMS, preferred_element_type=F32)
            s = jnp.where(valid, s - slope * dist_f, -jnp.inf)
            sink = sink_ref[h]
            m = jnp.maximum(jnp.max(s, axis=-1, keepdims=True), sink)
            p = jnp.exp(s - m)
            denom = jnp.sum(p, axis=-1, keepdims=True) + jnp.exp(sink - m)
            p = p / denom
            acc = acc + jnp.dot(p.astype(BF16), vvar[kh][sub], preferred_element_type=F32)
        o_ref[:, pair * 2 * half:(pair + 1) * 2 * half] = acc.astype(BF16)


def _swa(proj, sinks, bsz, seq, q_blk, k_blk, v_blk):
    n = bsz * seq
    nb = seq // WINDOW
    qw = ATT_HEADS * ATT_HEAD_DIM
    kw = ATT_KV_HEADS * ATT_HEAD_DIM
    cur = lambda col: (lambda b, i: (b * nb + i, col))
    prev = lambda col: (lambda b, i: (b * nb + jnp.maximum(i - 1, 0), col))
    return pl.pallas_call(
        _swa_kernel,
        out_shape=jax.ShapeDtypeStruct((n, qw), BF16),
        grid=(bsz, nb),
        in_specs=[pl.BlockSpec(memory_space=pltpu.SMEM),
                  pl.BlockSpec((WINDOW, qw), cur(q_blk)),
                  pl.BlockSpec((WINDOW, kw), prev(k_blk)),
                  pl.BlockSpec((WINDOW, kw), cur(k_blk)),
                  pl.BlockSpec((WINDOW, kw), prev(v_blk)),
                  pl.BlockSpec((WINDOW, kw), cur(v_blk))],
        out_specs=pl.BlockSpec((WINDOW, qw), lambda b, i: (b * nb + i, 0)),
        compiler_params=_cparams(("arbitrary", "arbitrary")),
        name="swa",
    )(sinks, proj, proj, proj, proj, proj)


def _gla_kernel(q_ref, k_ref, v_ref, r_ref, alr_ref, wg_ref, bg_ref, gain_ref, o_ref, st_scr):
    c = GLA_CHUNK
    hp = st_scr.shape[0]
    ts = q_ref.shape[0]
    dk = q_ref.shape[1] // hp
    dv = v_ref.shape[1] // hp
    nc = ts // c

    @pl.when(pl.program_id(2) == 0)
    def _():
        st_scr[...] = jnp.zeros_like(st_scr)

    z = jnp.dot(alr_ref[...], wg_ref[...], preferred_element_type=F32,
                precision=lax.Precision.HIGHEST) + bg_ref[...]
    log_a = (jnp.minimum(z, 0.0) - jnp.log1p(jnp.exp(-jnp.abs(z)))) / GLA_GATE_TEMP
    row = lax.broadcasted_iota(I32, (ts, ts), 0)
    col = lax.broadcasted_iota(I32, (ts, ts), 1)
    causal = (row >= col) & ((row // c) == (col // c))
    b = jnp.dot(causal.astype(F32), log_a, preferred_element_type=F32,
                precision=lax.Precision.HIGHEST)
    b_last = [b[(ci + 1) * c - 1:(ci + 1) * c, :] for ci in range(nc)]
    b_end = jnp.concatenate([jnp.broadcast_to(bl, (c, hp * dk)) for bl in b_last], axis=0)
    q = q_ref[...].astype(F32)
    k = k_ref[...].astype(F32)
    q_e = (q * jnp.exp(b) * (dk ** -0.5)).astype(BF16)
    k_e = (k * jnp.exp(-b)).astype(BF16)
    k_end = (k * jnp.exp(b_end - b)).astype(BF16)
    decay = [jnp.exp(bl) for bl in b_last]

    outs = []
    for h in range(hp):
        kc = slice(h * dk, (h + 1) * dk)
        v = v_ref[:, h * dv:(h + 1) * dv]
        a = lax.dot_general(q_e[:, kc], k_e[:, kc], NT_DIMS, preferred_element_type=F32)
        a = jnp.where(causal, a, 0.0).astype(BF16)
        o_intra = jnp.dot(a, v, preferred_element_type=F32)
        st = st_scr[h]
        o_inter = []
        for ci in range(nc):
            rows = slice(ci * c, (ci + 1) * c)
            o_inter.append(lax.dot_general(q_e[rows, kc], st.astype(BF16), NT_DIMS,
                                           preferred_element_type=F32))
            upd = lax.dot_general(v[rows, :], k_end[rows, kc], TN_DIMS,
                                  preferred_element_type=F32)
            st = st * decay[ci][:, kc] + upd
        st_scr[h] = st
        o = o_intra + jnp.concatenate(o_inter, axis=0)
        outs.append(o * lax.rsqrt(jnp.mean(o * o, axis=-1, keepdims=True) + LN_EPS))

    o = jnp.concatenate(outs, axis=1) * gain_ref[...]
    r = r_ref[...].astype(F32)
    o_ref[...] = (o * (r * _sigmoid(r))).astype(BF16)


def _gla(proj, alr, wg_pad, bg, gain, bsz, seq, q_blk, k_blk, v_blk, r_blk):
    n = bsz * seq
    hp = GLA_HEADS_PER_STEP
    dk = wg_pad.shape[1] // GLA_HEADS
    dv = gain.shape[1] // GLA_HEADS
    ts = min(GLA_STEP, seq)
    steps = seq // ts
    assert q_blk % hp == 0 and k_blk % hp == 0 and v_blk % hp == 0 and r_blk % hp == 0
    rowmap = lambda col0: (lambda b, h, t: (b * steps + t, col0 // hp + h))
    return pl.pallas_call(
        _gla_kernel,
        out_shape=jax.ShapeDtypeStruct((n, GLA_HEADS * dv), BF16),
        grid=(bsz, GLA_HEADS // hp, steps),
        in_specs=[pl.BlockSpec((ts, hp * dk), rowmap(q_blk)),
                  pl.BlockSpec((ts, hp * dk), rowmap(k_blk)),
                  pl.BlockSpec((ts, hp * dv), rowmap(v_blk)),
                  pl.BlockSpec((ts, hp * dv), rowmap(r_blk)),
                  pl.BlockSpec((ts, LANES), lambda b, h, t: (b * steps + t, 0)),
                  pl.BlockSpec((LANES, hp * dk), lambda b, h, t: (0, h)),
                  pl.BlockSpec((1, hp * dk), lambda b, h, t: (0, h)),
                  pl.BlockSpec((1, hp * dv), lambda b, h, t: (0, h))],
        out_specs=pl.BlockSpec((ts, hp * dv), rowmap(0)),
        scratch_shapes=[pltpu.VMEM((hp, dv, dk), F32)],
        compiler_params=_cparams(("arbitrary", "arbitrary", "arbitrary")),
        name="gla",
    )(proj, proj, proj, proj, alr, wg_pad, bg, gain)


def _merge_kernel(ya_ref, yg_ref, ga_ref, gg_ref, x_ref, gt1_ref, sc2_ref, sh2_ref,
                  pa_ref, pg_ref, wo_ref, g1_ref, b1_ref, wrh_ref, wrl_ref, br_ref,
                  x1_ref, h2_ref, lt_ref):
    a = jnp.dot(ya_ref[...], pa_ref[...], preferred_element_type=F32)
    g = jnp.dot(yg_ref[...], pg_ref[...], preferred_element_type=F32)
    merged = (_sigmoid(ga_ref[...].astype(F32)) * a + _sigmoid(gg_ref[...].astype(F32)) * g)
    mix = jnp.dot(merged.astype(BF16), wo_ref[...], preferred_element_type=F32)
    x1 = _ln(DN_ALPHA * x_ref[...] + gt1_ref[...] * mix) * g1_ref[...] + b1_ref[...]
    x1_ref[...] = x1
    h2 = _ln(x1) * (1.0 + sc2_ref[...]) + sh2_ref[...]
    h2_ref[...] = h2
    h2_hi = h2.astype(BF16)
    h2_lo = (h2 - h2_hi.astype(F32)).astype(BF16)
    wrh = wrh_ref[...]
    logits = (lax.dot_general(wrh, h2_hi, NT_DIMS, preferred_element_type=F32)
              + lax.dot_general(wrh, h2_lo, NT_DIMS, preferred_element_type=F32)
              + lax.dot_general(wrl_ref[...], h2_hi, NT_DIMS, preferred_element_type=F32))
    lt_ref[...] = logits + br_ref[...]


def _merge(y_att, y_gla, proj, x2, ada3, p_a, p_g, w_o, g1, b1, wr_hi, wr_lo, b_r, seq,
           ga_blk, gg_blk):
    n, d = x2.shape
    tm = min(MERGE_TM, seq)
    per_b = seq // tm
    ne = wr_hi.shape[0]
    const = lambda shape: pl.BlockSpec(shape, lambda i: (0,) * len(shape),
                                       pipeline_mode=pl.Buffered(1))
    adarow = lambda k: pl.BlockSpec((None, 1, d), lambda i: ((i // per_b) * 6 + k, 0, 0))
    return pl.pallas_call(
        _merge_kernel,
        out_shape=(jax.ShapeDtypeStruct((n, d), F32),
                   jax.ShapeDtypeStruct((n, d), F32),
                   jax.ShapeDtypeStruct((ne, n), F32)),
        grid=(n // tm,),
        in_specs=[pl.BlockSpec((tm, y_att.shape[1]), lambda i: (i, 0)),
                  pl.BlockSpec((tm, d), lambda i: (i, 0)),
                  pl.BlockSpec((tm, d), lambda i: (i, ga_blk)),
                  pl.BlockSpec((tm, d), lambda i: (i, gg_blk)),
                  pl.BlockSpec((tm, d), lambda i: (i, 0)),
                  adarow(2), adarow(4), adarow(3),
                  const(p_a.shape), const(p_g.shape), const(w_o.shape),
                  const((1, d)), const((1, d)),
                  const(wr_hi.shape), const(wr_lo.shape), const((ne, 1))],
        out_specs=(pl.BlockSpec((tm, d), lambda i: (i, 0)),
                   pl.BlockSpec((tm, d), lambda i: (i, 0)),
                   pl.BlockSpec((ne, tm), lambda i: (0, i))),
        compiler_params=_cparams(("arbitrary",)),
        name="merge",
    )(y_att, y_gla, proj, proj, x2, ada3, ada3, ada3, p_a, p_g, w_o, g1, b1, wr_hi, wr_lo, b_r)


def _route_kernel(lt_ref, dest_ref, w_ref, blke_ref, bend_ref, oh_scr, rank_scr):
    ne, n = lt_ref.shape
    logits = lt_ref[...]
    eidx = lax.broadcasted_iota(I32, (ne, n), 0).astype(F32)
    vals, idxs = [], []
    for _ in range(TOP_K):
        m = jnp.max(logits, axis=0, keepdims=True)
        idx = jnp.min(jnp.where(logits == m, eidx, float(ne)), axis=0, keepdims=True)
        vals.append(m)
        idxs.append(idx)
        logits = jnp.where(eidx == idx, -jnp.inf, logits)
    exps = [jnp.exp(v - vals[0]) for v in vals]
    total = exps[0] + exps[1] + exps[2] + exps[3]
    for k in range(TOP_K):
        w_ref[k:k + 1, :] = exps[k] / total

    onehot = jnp.zeros((ne, n), F32)
    for k in range(TOP_K):
        onehot = onehot + (eidx == idxs[k]).astype(F32)
    oh_scr[...] = onehot.astype(BF16)

    r = lax.broadcasted_iota(I32, (LANES, 2 * LANES), 0)
    cidx = lax.broadcasted_iota(I32, (LANES, 2 * LANES), 1)
    scan_mat = ((r < cidx) | (cidx >= LANES)).astype(BF16)
    carry = jnp.zeros((ne, LANES), F32)
    for t in range(n // LANES):
        cols = slice(t * LANES, (t + 1) * LANES)
        both = jnp.dot(oh_scr[:, cols], scan_mat, preferred_element_type=F32)
        rank_scr[:, cols] = both[:, :LANES] + carry
        carry = carry + both[:, LANES:]

    counts = carry
    nblk = jnp.floor((counts + (MOE_BLOCK - 1)) * (1.0 / MOE_BLOCK))
    er = lax.broadcasted_iota(I32, (ne, ne), 0)
    ec = lax.broadcasted_iota(I32, (ne, ne), 1)
    strict_lower = (ec < er).astype(BF16)
    blk_start = jnp.dot(strict_lower, nblk.astype(BF16), preferred_element_type=F32)
    blk_end = blk_start + nblk
    row_start = blk_start[:, :1] * float(MOE_BLOCK)

    pos = rank_scr[...] + row_start
    for k in range(TOP_K):
        d = jnp.sum(jnp.where(eidx == idxs[k], pos, 0.0), axis=0, keepdims=True)
        dest_ref[k:k + 1, :] = d.astype(I32)

    nb_pad = blke_ref.shape[1]
    j = lax.broadcasted_iota(I32, (ne, nb_pad), 1).astype(F32)
    blke = jnp.sum((blk_end[:, :1] <= j).astype(F32), axis=0, keepdims=True)
    blke_ref[...] = jnp.minimum(blke, float(ne - 1)).astype(I32)
    bend_ref[...] = blk_end.astype(I32)


def _route(logits_t, n_blocks):
    ne, n = logits_t.shape
    nb_pad = -(-n_blocks // LANES) * LANES
    return pl.pallas_call(
        _route_kernel,
        out_shape=(jax.ShapeDtypeStruct((TOP_K, n), I32),
                   jax.ShapeDtypeStruct((TOP_K, n), F32),
                   jax.ShapeDtypeStruct((1, nb_pad), I32),
                   jax.ShapeDtypeStruct((ne, LANES), I32)),
        scratch_shapes=[pltpu.VMEM((ne, n), BF16), pltpu.VMEM((ne, n), F32)],
        compiler_params=pltpu.CompilerParams(vmem_limit_bytes=VMEM_LIMIT),
        name="route",
    )(logits_t)


def _dispatch_kernel(dest_ref, bend_ref, h_ref, xs_ref, zero_scr, sem, zsem):
    tm = h_ref.shape[0]
    n = dest_ref.shape[0] // TOP_K
    ne = bend_ref.shape[0]
    base = pl.program_id(0) * tm

    @pl.when(pl.program_id(0) == 0)
    def _():
        zero_scr[...] = jnp.zeros_like(zero_scr)

        def zero_block(blk):
            start = pl.multiple_of(blk * MOE_BLOCK, MOE_BLOCK)
            return pltpu.make_async_copy(zero_scr, xs_ref.at[pl.ds(start, MOE_BLOCK), :], zsem)

        def nonempty(e):
            return bend_ref[e] > (bend_ref[e - 1] if e > 0 else 0)

        for e in range(ne):
            @pl.when(nonempty(e))
            def _():
                zero_block(bend_ref[e] - 1).start()
        for e in range(ne):
            @pl.when(nonempty(e))
            def _():
                zero_block(bend_ref[e] - 1).wait()

        n_blocks = xs_ref.shape[0] // MOE_BLOCK

        def start_tail(blk, carry):
            zero_block(blk).start()
            return carry

        def wait_tail(blk, carry):
            zero_block(blk).wait()
            return carry

        lax.fori_loop(bend_ref[ne - 1], n_blocks, start_tail, 0)
        lax.fori_loop(bend_ref[ne - 1], n_blocks, wait_tail, 0)

    def row_copy(r, d):
        return pltpu.make_async_copy(h_ref.at[pl.ds(r, 1), :], xs_ref.at[pl.ds(d, 1), :], sem)

    def issue(g, carry):
        r0 = pl.multiple_of(g * ROW_UNROLL, ROW_UNROLL)
        for u in range(ROW_UNROLL):
            for k in range(TOP_K):
                row_copy(r0 + u, dest_ref[k * n + base + r0 + u]).start()
        return carry

    def drain(g, carry):
        for _ in range(ROW_UNROLL * TOP_K):
            row_copy(0, 0).wait()
        return carry

    lax.fori_loop(0, tm // ROW_UNROLL, issue, 0)
    lax.fori_loop(0, tm // ROW_UNROLL, drain, 0)


def _dispatch(dest_flat, blk_end, h2, cap):
    n, d = h2.shape
    tm = min(ROW_TM, n)
    return pl.pallas_call(
        _dispatch_kernel,
        out_shape=jax.ShapeDtypeStruct((cap, d), F32),
        grid_spec=pltpu.PrefetchScalarGridSpec(
            num_scalar_prefetch=2,
            grid=(n // tm,),
            in_specs=[pl.BlockSpec((tm, d), lambda i, dest, bend: (i, 0))],
            out_specs=pl.BlockSpec(memory_space=pl.ANY),
            scratch_shapes=[pltpu.VMEM((MOE_BLOCK, d), F32),
                            pltpu.SemaphoreType.DMA(()),
                            pltpu.SemaphoreType.DMA(())]),
        compiler_params=_cparams(("arbitrary",)),
        name="dispatch",
    )(dest_flat, blk_end, h2)


GU_CHUNK = 2 * LANES
EXPERT_RING = 5


def _expert_kernel(blke_ref, bend_ref, xs_ref, wgu_ref, wd_ref, bg_ref, bu_ref, bd_ref,
                   ys_ref, wg_scr, wu_scr, wd_scr, gu_stage, wd_stage, sems):
    j = pl.program_id(0)
    n_blocks = pl.num_programs(0)
    ne = bend_ref.shape[0]
    n_used = bend_ref[ne - 1]
    e = blke_ref[j]
    e_end = bend_ref[e]
    d, f = wg_scr.shape
    units = f // LANES
    ring = gu_stage.shape[0]
    tf = min(EXPERT_TF, f)
    first = (j == 0) | (e != blke_ref[jnp.maximum(j - 1, 0)])
    last = j == e_end - 1
    has_next = e_end < n_used
    nxt = blke_ref[jnp.minimum(e_end, n_blocks - 1)]

    def gu_copy(ex, u, slot):
        return pltpu.make_async_copy(wgu_ref.at[ex, :, pl.ds(u * GU_CHUNK, GU_CHUNK)],
                                     gu_stage.at[slot], sems.at[0, slot])

    def wd_copy(ex, u, slot):
        return pltpu.make_async_copy(wd_ref.at[ex, pl.ds(u * LANES, LANES), :],
                                     wd_stage.at[slot], sems.at[1, slot])

    def start_unit(ex, u):
        gu_copy(ex, u, u % ring).start()
        wd_copy(ex, u, u % ring).start()

    def convert_unit(ex, u):
        slot = u % ring
        gu_copy(ex, u, slot).wait()
        wd_copy(ex, u, slot).wait()
        src = lax.broadcasted_iota(I32, (GU_CHUNK, GU_CHUNK), 0)
        dst = lax.broadcasted_iota(I32, (GU_CHUNK, GU_CHUNK), 1)
        perm = (dst == (src >> 1) + (src & 1) * LANES).astype(BF16)
        cols = slice(u * LANES, (u + 1) * LANES)
        sep = jnp.dot(gu_stage[slot].astype(BF16), perm, preferred_element_type=F32)
        wg_scr[:, cols] = sep[:, :LANES].astype(BF16)
        wu_scr[:, cols] = sep[:, LANES:].astype(BF16)
        wd_scr[cols, :] = wd_stage[slot].astype(BF16)
        if u + ring < units:
            start_unit(ex, u + ring)

    @pl.when(j == 0)
    def _():
        for u in range(ring):
            start_unit(e, u)
        for u in range(units):
            convert_unit(e, u)

    @pl.when((j < n_used) & first & has_next)
    def _():
        for u in range(ring):
            start_unit(nxt, u)

    def compute(replace):
        x = xs_ref[...].astype(BF16)
        acc = jnp.zeros(ys_ref.shape, F32)
        for c in range(f // tf):
            cols = slice(c * tf, (c + 1) * tf)
            g = jnp.dot(x, wg_scr[:, cols], preferred_element_type=F32) + bg_ref[:, cols]
            u = jnp.dot(x, wu_scr[:, cols], preferred_element_type=F32) + bu_ref[:, cols]
            gate = jnp.minimum(g, SWIGLU_LIMIT)
            up = jnp.clip(u, -SWIGLU_LIMIT, SWIGLU_LIMIT)
            act = (up + 1.0) * gate * _sigmoid(SWIGLU_ALPHA * gate)
            acc = acc + jnp.dot(act.astype(BF16), wd_scr[cols, :], preferred_element_type=F32)
            if replace:
                for unit in range(c * tf // LANES, (c + 1) * tf // LANES):
                    convert_unit(nxt, unit)
        ys_ref[...] = acc + bd_ref[...]

    @pl.when((j < n_used) & last & has_next)
    def _():
        compute(True)

    @pl.when((j < n_used) & jnp.logical_not(last & has_next))
    def _():
        compute(False)

    @pl.when(j >= n_used)
    def _():
        ys_ref[...] = jnp.zeros_like(ys_ref)


def _experts(blk_e, blk_end, xs, w_gate_up, w_down, b_gate, b_up, b_down):
    cap, d = xs.shape
    ne, f, _ = w_down.shape
    n_blocks = cap // MOE_BLOCK
    ring = min(EXPERT_RING, f // LANES)
    bspec = lambda width: pl.BlockSpec((None, 1, width), lambda j, be, bend: (be[j], 0, 0))
    xs_map = lambda j, be, bend: (jnp.minimum(j, bend[ne - 1] - 1), 0)
    return pl.pallas_call(
        _expert_kernel,
        out_shape=jax.ShapeDtypeStruct((cap, d), F32),
        grid_spec=pltpu.PrefetchScalarGridSpec(
            num_scalar_prefetch=2,
            grid=(n_blocks,),
            in_specs=[pl.BlockSpec((MOE_BLOCK, d), xs_map),
                      pl.BlockSpec(memory_space=pl.ANY),
                      pl.BlockSpec(memory_space=pl.ANY),
                      bspec(f), bspec(f), bspec(d)],
            out_specs=pl.BlockSpec((MOE_BLOCK, d), lambda j, be, bend: (j, 0)),
            scratch_shapes=[pltpu.VMEM((d, f), BF16), pltpu.VMEM((d, f), BF16),
                            pltpu.VMEM((f, d), BF16),
                            pltpu.VMEM((ring, d, GU_CHUNK), F32), pltpu.VMEM((ring, LANES, d), F32),
                            pltpu.SemaphoreType.DMA((2, ring))]),
        compiler_params=_cparams(("arbitrary",)),
        name="experts",
    )(blk_e, blk_end, xs, w_gate_up, w_down, b_gate, b_up, b_down)


def _combine_kernel(dest_ref, ys_ref, wt_ref, x1_ref, gt2_ref, g2_ref, b2_ref, o_ref, buf, sem):
    tm = x1_ref.shape[0]
    n = dest_ref.shape[0] // TOP_K
    base = pl.program_id(0) * tm

    def row_copy(k, r, d):
        return pltpu.make_async_copy(ys_ref.at[pl.ds(d, 1), :], buf.at[k, pl.ds(r, 1), :], sem)

    def issue(g, carry):
        r0 = pl.multiple_of(g * ROW_UNROLL, ROW_UNROLL)
        for u in range(ROW_UNROLL):
            for k in range(TOP_K):
                row_copy(k, r0 + u, dest_ref[k * n + base + r0 + u]).start()
        return carry

    def drain(g, carry):
        for _ in range(ROW_UNROLL * TOP_K):
            row_copy(0, 0, 0).wait()
        return carry

    lax.fori_loop(0, tm // ROW_UNROLL, issue, 0)
    lax.fori_loop(0, tm // ROW_UNROLL, drain, 0)

    wt = wt_ref[...]
    ffn = buf[0] * wt[:, 0:1]
    for k in range(1, TOP_K):
        ffn = ffn + buf[k] * wt[:, k:k + 1]
    z = DN_ALPHA * x1_ref[...] + gt2_ref[...] * ffn
    o_ref[...] = _ln(z) * g2_ref[...] + b2_ref[...]


def _combine(dest_flat, ys, w_t, x1, ada3, g2, b2, seq):
    n, d = x1.shape
    tm = min(ROW_TM, seq)
    per_b = seq // tm
    return pl.pallas_call(
        _combine_kernel,
        out_shape=jax.ShapeDtypeStruct((n, d), F32),
        grid_spec=pltpu.PrefetchScalarGridSpec(
            num_scalar_prefetch=1,
            grid=(n // tm,),
            in_specs=[pl.BlockSpec(memory_space=pl.ANY),
                      pl.BlockSpec((tm, TOP_K), lambda i, dest: (i, 0)),
                      pl.BlockSpec((tm, d), lambda i, dest: (i, 0)),
                      pl.BlockSpec((None, 1, d), lambda i, dest: ((i // per_b) * 6 + 5, 0, 0)),
                      pl.BlockSpec((1, d), lambda i, dest: (0, 0)),
                      pl.BlockSpec((1, d), lambda i, dest: (0, 0))],
            out_specs=pl.BlockSpec((tm, d), lambda i, dest: (i, 0)),
            scratch_shapes=[pltpu.VMEM((TOP_K, tm, d), F32),
                            pltpu.SemaphoreType.DMA(())]),
        compiler_params=_cparams(("arbitrary",)),
        name="combine",
    )(dest_flat, ys, w_t, x1, ada3, g2, b2)


def kernel(x, c, w_ada, b_ada, w_in, w_gla_gate_up, b_gla_gate, attn_sinks, gla_norm_gain,
           w_branch_att, w_branch_gla, w_out, ln1_gain, ln1_bias, w_router, b_router,
           w_gate_up, b_gate_up, w_down, b_down, ln2_gain, ln2_bias):
    bsz, seq, d = x.shape
    n = bsz * seq
    qa_w = ATT_HEADS * ATT_HEAD_DIM
    kv_w = ATT_KV_HEADS * ATT_HEAD_DIM
    gk_w = d // 2
    splits = (qa_w, kv_w, kv_w, gk_w, gk_w, d, d, GLA_GATE_RANK, d, d)
    offs = [0]
    for wdt in splits:
        offs.append(offs[-1] + wdt)
    ne = w_router.shape[-1]
    f = w_down.shape[-2]

    x2 = x.reshape(n, d)
    for l in range(w_in.shape[0]):
        wl = w_in[l]
        part = lambda i: wl[:, offs[i]:offs[i + 1]]
        w_main = jnp.concatenate([part(3), part(4), part(5), part(6), part(8), part(9),
                                  part(0), part(1), part(2)], axis=1).astype(BF16)
        w_alr = jnp.pad(part(7), ((0, 0), (0, LANES - GLA_GATE_RANK))).astype(BF16)
        wg_pad = jnp.pad(w_gla_gate_up[l], ((0, LANES - GLA_GATE_RANK), (0, 0)))
        wr_t = w_router[l].T
        wr_hi = wr_t.astype(BF16)
        wr_lo = (wr_t - wr_hi.astype(F32)).astype(BF16)
        bgu = b_gate_up[l].reshape(ne, 1, f, 2)

        ada = _ada(c, w_ada[l], b_ada[l])
        ada3 = ada.reshape(bsz * 6, 1, d)

        proj, alr = _proj(x2, ada3, w_main, w_alr, seq)
        y_att = _swa(proj, attn_sinks[l], bsz, seq,
                     q_blk=(2 * gk_w + 4 * d) // qa_w,
                     k_blk=(2 * gk_w + 4 * d + qa_w) // kv_w,
                     v_blk=(2 * gk_w + 4 * d + qa_w + kv_w) // kv_w)
        dk = gk_w // GLA_HEADS
        dv = d // GLA_HEADS
        y_gla = _gla(proj, alr, wg_pad, b_gla_gate[l].reshape(1, gk_w),
                     gla_norm_gain[l].reshape(1, d), bsz, seq,
                     q_blk=0, k_blk=gk_w // dk, v_blk=2 * gk_w // dv, r_blk=(2 * gk_w + d) // dv)
        x1, h2, logits_t = _merge(
            y_att, y_gla, proj, x2, ada3,
            w_branch_att[l].astype(BF16), w_branch_gla[l].astype(BF16), w_out[l].astype(BF16),
            ln1_gain[l].reshape(1, d), ln1_bias[l].reshape(1, d),
            wr_hi, wr_lo, b_router[l].reshape(ne, 1), seq,
            ga_blk=(2 * gk_w + 2 * d) // d, gg_blk=(2 * gk_w + 3 * d) // d)

        n_rows = n * TOP_K
        cap = -(-n_rows // MOE_BLOCK) * MOE_BLOCK + ne * MOE_BLOCK
        n_blocks = cap // MOE_BLOCK
        dest, w_top, blk_e, blk_end = _route(logits_t, n_blocks)
        dest_flat = dest.reshape(-1)
        blk_e = blk_e.reshape(-1)[:n_blocks]
        blk_end = blk_end[:, 0]
        xs = _dispatch(dest_flat, blk_end, h2, cap)
        ys = _experts(blk_e, blk_end, xs, w_gate_up[l], w_down[l],
                      bgu[..., 0], bgu[..., 1], b_down[l].reshape(ne, 1, d))
        x2 = _combine(dest_flat, ys, w_top.T, x1, ada3,
                      ln2_gain[l].reshape(1, d), ln2_bias[l].reshape(1, d), seq)
    return x2.reshape(bsz, seq, d)
```

```python
import jax
import jax.numpy as jnp
from jax import lax
from jax.experimental import pallas as pl
from jax.experimental.pallas import tpu as pltpu

F32 = jnp.float32
BF16 = jnp.bfloat16
I32 = jnp.int32

ATT_HEADS = 16
ATT_KV_HEADS = 2
ATT_HEAD_DIM = 64
WINDOW = 128
GLA_HEADS = 4
GLA_GATE_RANK = 16
GLA_GATE_TEMP = 16.0
GLA_CHUNK = 64
N_EXPERTS = 32
TOP_K = 4
SWIGLU_LIMIT = 7.0
SWIGLU_ALPHA = 1.702
LN_EPS = 1e-5
DEPTH = 1
DN_ALPHA = (2 * DEPTH) ** 0.25

LANES = 128
VMEM_LIMIT = 56 * 1024 * 1024

MOE_BLOCK = 256
PROJ_TM = 1024
PROJ_TN = 1152
GLA_STEP = 256
GLA_HEADS_PER_STEP = 4
MERGE_TM = 256
ROW_TM = 256
ROW_UNROLL = 8
INVERT_STEPS = 32
EXPERT_TF = 512
LN_ROWS = 256

NT_DIMS = (((1,), (1,)), ((), ()))
TN_DIMS = (((0,), (0,)), ((), ()))


def _cparams(sem, vmem=VMEM_LIMIT):
    return pltpu.CompilerParams(dimension_semantics=sem, vmem_limit_bytes=vmem)


def _ln(x):
    mu = jnp.mean(x, axis=-1, keepdims=True)
    xc = x - mu
    var = jnp.mean(xc * xc, axis=-1, keepdims=True)
    return xc * lax.rsqrt(var + LN_EPS)


def _sigmoid(x):
    return 1.0 / (1.0 + jnp.exp(-x))


def _ada_kernel(c_ref, w_ref, b_ref, o_ref):
    c = c_ref[...]
    s = c * _sigmoid(c)
    o_ref[...] = jnp.dot(s.astype(BF16), w_ref[...].astype(BF16),
                         preferred_element_type=F32) + b_ref[...]


def _ada(c, w_ada, b_ada):
    bsz, d = c.shape
    n = w_ada.shape[1]
    tn = 1536
    return pl.pallas_call(
        _ada_kernel,
        out_shape=jax.ShapeDtypeStruct((bsz, n), F32),
        grid=(n // tn,),
        in_specs=[pl.BlockSpec((bsz, d), lambda j: (0, 0)),
                  pl.BlockSpec((d, tn), lambda j: (0, j)),
                  pl.BlockSpec((1, tn), lambda j: (0, j))],
        out_specs=pl.BlockSpec((bsz, tn), lambda j: (0, j)),
        compiler_params=_cparams(("arbitrary",)),
        name="ada",
    )(c, w_ada, b_ada.reshape(1, n))


def _proj_kernel(x_ref, sc_ref, sh_ref, w_ref, walr_ref, o_ref, alr_ref, h_scr):
    @pl.when(pl.program_id(1) == 0)
    def _():
        tm = x_ref.shape[0]
        sub = min(LN_ROWS, tm)
        for r0 in range(0, tm, sub):
            rows = slice(r0, r0 + sub)
            h = _ln(x_ref[rows, :]) * (1.0 + sc_ref[...]) + sh_ref[...]
            h_scr[rows, :] = h.astype(BF16)
        alr_ref[...] = jnp.dot(h_scr[...], walr_ref[...], preferred_element_type=F32)

    o_ref[...] = jnp.dot(h_scr[...], w_ref[...],
                         preferred_element_type=F32).astype(BF16)


def _proj(x2, ada3, w_main, w_alr, seq):
    n, d = x2.shape
    cols = w_main.shape[1]
    tm, tn = min(PROJ_TM, seq), PROJ_TN
    per_b = seq // tm
    return pl.pallas_call(
        _proj_kernel,
        out_shape=(jax.ShapeDtypeStruct((n, cols), BF16),
                   jax.ShapeDtypeStruct((n, LANES), F32)),
        grid=(n // tm, cols // tn),
        in_specs=[pl.BlockSpec((tm, d), lambda i, j: (i, 0)),
                  pl.BlockSpec((None, 1, d), lambda i, j: ((i // per_b) * 6 + 1, 0, 0)),
                  pl.BlockSpec((None, 1, d), lambda i, j: ((i // per_b) * 6 + 0, 0, 0)),
                  pl.BlockSpec((d, tn), lambda i, j: (0, j)),
                  pl.BlockSpec((d, LANES), lambda i, j: (0, 0))],
        out_specs=(pl.BlockSpec((tm, tn), lambda i, j: (i, j)),
                   pl.BlockSpec((tm, LANES), lambda i, j: (i, 0))),
        scratch_shapes=[pltpu.VMEM((tm, d), BF16)],
        compiler_params=_cparams(("arbitrary", "arbitrary")),
        name="proj",
    )(x2, ada3, ada3, w_main, w_alr)


def _swa_kernel(sink_ref, q_ref, kp_ref, kc_ref, vp_ref, vc_ref, o_ref):
    n = pl.program_id(1)
    blk = WINDOW
    half = ATT_HEAD_DIM
    group = ATT_HEADS // ATT_KV_HEADS

    lane = lax.broadcasted_iota(I32, (2 * blk, 2 * half), 1)
    lo = lane < half

    def variants(prev_ref, cur_ref, scale):
        band = jnp.concatenate([prev_ref[...], cur_ref[...]], axis=0).astype(F32) * scale
        rolled = pltpu.roll(band, half, axis=1)
        zero = jnp.zeros_like(band)
        return ((jnp.where(lo, band, zero).astype(BF16), jnp.where(lo, zero, rolled).astype(BF16)),
                (jnp.where(lo, rolled, zero).astype(BF16), jnp.where(lo, zero, band).astype(BF16)))

    kvar = variants(kp_ref, kc_ref, ATT_HEAD_DIM ** -0.5)
    vvar = variants(vp_ref, vc_ref, 1.0)

    qi = lax.broadcasted_iota(I32, (blk, 2 * blk), 0)
    si = lax.broadcasted_iota(I32, (blk, 2 * blk), 1)
    dist = qi + blk - si
    first_key = jnp.where(n > 0, 0, blk)
    valid = (dist >= 0) & (dist < WINDOW) & (si >= first_key)
    dist_f = dist.astype(F32)

    for pair in range(ATT_HEADS // 2):
        q_pair = q_ref[:, pair * 2 * half:(pair + 1) * 2 * half]
        acc = jnp.zeros((blk, 2 * half), F32)
        for sub in range(2):
            h = pair * 2 + sub
            kh = h // group
            slope = 2.0 ** (-8.0 * (h + 1.0) / ATT_HEADS)
            s = lax.dot_general(q_pair, kvar[kh][sub], NT_DIMS, preferred_element_type=F32)
            s = jnp.where(valid, s - slope * dist_f, -jnp.inf)
            sink = sink_ref[h]
            m = jnp.maximum(jnp.max(s, axis=-1, keepdims=True), sink)
            p = jnp.exp(s - m)
            denom = jnp.sum(p, axis=-1, keepdims=True) + jnp.exp(sink - m)
            p = p / denom
            acc = acc + jnp.dot(p.astype(BF16), vvar[kh][sub], preferred_element_type=F32)
        o_ref[:, pair * 2 * half:(pair + 1) * 2 * half] = acc.astype(BF16)


def _swa(proj, sinks, bsz, seq, q_blk, k_blk, v_blk):
    n = bsz * seq
    nb = seq // WINDOW
    qw = ATT_HEADS * ATT_HEAD_DIM
    kw = ATT_KV_HEADS * ATT_HEAD_DIM
    cur = lambda col: (lambda b, i: (b * nb + i, col))
    prev = lambda col: (lambda b, i: (b * nb + jnp.maximum(i - 1, 0), col))
    return pl.pallas_call(
        _swa_kernel,
        out_shape=jax.ShapeDtypeStruct((n, qw), BF16),
        grid=(bsz, nb),
        in_specs=[pl.BlockSpec(memory_space=pltpu.SMEM),
                  pl.BlockSpec((WINDOW, qw), cur(q_blk)),
                  pl.BlockSpec((WINDOW, kw), prev(k_blk)),
                  pl.BlockSpec((WINDOW, kw), cur(k_blk)),
                  pl.BlockSpec((WINDOW, kw), prev(v_blk)),
                  pl.BlockSpec((WINDOW, kw), cur(v_blk))],
        out_specs=pl.BlockSpec((WINDOW, qw), lambda b, i: (b * nb + i, 0)),
        compiler_params=_cparams(("arbitrary", "arbitrary")),
        name="swa",
    )(sinks, proj, proj, proj, proj, proj)


def _gla_kernel(q_ref, k_ref, v_ref, r_ref, alr_ref, wg_ref, bg_ref, gain_ref, o_ref, st_scr):
    c = GLA_CHUNK
    hp = st_scr.shape[0]
    ts = q_ref.shape[0]
    dk = q_ref.shape[1] // hp
    dv = v_ref.shape[1] // hp
    nc = ts // c

    @pl.when(pl.program_id(2) == 0)
    def _():
        st_scr[...] = jnp.zeros_like(st_scr)

    z = jnp.dot(alr_ref[...], wg_ref[...], preferred_element_type=F32,
                precision=lax.Precision.HIGHEST) + bg_ref[...]
    log_a = (jnp.minimum(z, 0.0) - jnp.log1p(jnp.exp(-jnp.abs(z)))) / GLA_GATE_TEMP
    row = lax.broadcasted_iota(I32, (ts, ts), 0)
    col = lax.broadcasted_iota(I32, (ts, ts), 1)
    causal = (row >= col) & ((row // c) == (col // c))
    b = jnp.dot(causal.astype(F32), log_a, preferred_element_type=F32,
                precision=lax.Precision.HIGHEST)
    b_last = [b[(ci + 1) * c - 1:(ci + 1) * c, :] for ci in range(nc)]
    b_end = jnp.concatenate([jnp.broadcast_to(bl, (c, hp * dk)) for bl in b_last], axis=0)
    q = q_ref[...].astype(F32)
    k = k_ref[...].astype(F32)
    q_e = (q * jnp.exp(b) * (dk ** -0.5)).astype(BF16)
    k_e = (k * jnp.exp(-b)).astype(BF16)
    k_end = (k * jnp.exp(b_end - b)).astype(BF16)
    decay = [jnp.exp(bl) for bl in b_last]

    outs = []
    for h in range(hp):
        kc = slice(h * dk, (h + 1) * dk)
        v = v_ref[:, h * dv:(h + 1) * dv]
        a = lax.dot_general(q_e[:, kc], k_e[:, kc], NT_DIMS, preferred_element_type=F32)
        a = jnp.where(causal, a, 0.0).astype(BF16)
        o_intra = jnp.dot(a, v, preferred_element_type=F32)
        st = st_scr[h]
        o_inter = []
        for ci in range(nc):
            rows = slice(ci * c, (ci + 1) * c)
            o_inter.append(lax.dot_general(q_e[rows, kc], st.astype(BF16), NT_DIMS,
                                           preferred_element_type=F32))
            upd = lax.dot_general(v[rows, :], k_end[rows, kc], TN_DIMS,
                                  preferred_element_type=F32)
            st = st * decay[ci][:, kc] + upd
        st_scr[h] = st
        o = o_intra + jnp.concatenate(o_inter, axis=0)
        outs.append(o * lax.rsqrt(jnp.mean(o * o, axis=-1, keepdims=True) + LN_EPS))

    o = jnp.concatenate(outs, axis=1) * gain_ref[...]
    r = r_ref[...].astype(F32)
    o_ref[...] = (o * (r * _sigmoid(r))).astype(BF16)


def _gla(proj, alr, wg_pad, bg, gain, bsz, seq, q_blk, k_blk, v_blk, r_blk):
    n = bsz * seq
    hp = GLA_HEADS_PER_STEP
    dk = wg_pad.shape[1] // GLA_HEADS
    dv = gain.shape[1] // GLA_HEADS
    ts = min(GLA_STEP, seq)
    steps = seq // ts
    assert q_blk % hp == 0 and k_blk % hp == 0 and v_blk % hp == 0 and r_blk % hp == 0
    rowmap = lambda col0: (lambda b, h, t: (b * steps + t, col0 // hp + h))
    return pl.pallas_call(
        _gla_kernel,
        out_shape=jax.ShapeDtypeStruct((n, GLA_HEADS * dv), BF16),
        grid=(bsz, GLA_HEADS // hp, steps),
        in_specs=[pl.BlockSpec((ts, hp * dk), rowmap(q_blk)),
                  pl.BlockSpec((ts, hp * dk), rowmap(k_blk)),
                  pl.BlockSpec((ts, hp * dv), rowmap(v_blk)),
                  pl.BlockSpec((ts, hp * dv), rowmap(r_blk)),
                  pl.BlockSpec((ts, LANES), lambda b, h, t: (b * steps + t, 0)),
                  pl.BlockSpec((LANES, hp * dk), lambda b, h, t: (0, h)),
                  pl.BlockSpec((1, hp * dk), lambda b, h, t: (0, h)),
                  pl.BlockSpec((1, hp * dv), lambda b, h, t: (0, h))],
        out_specs=pl.BlockSpec((ts, hp * dv), rowmap(0)),
        scratch_shapes=[pltpu.VMEM((hp, dv, dk), F32)],
        compiler_params=_cparams(("arbitrary", "arbitrary", "arbitrary")),
        name="gla",
    )(proj, proj, proj, proj, alr, wg_pad, bg, gain)


def _merge_kernel(ya_ref, yg_ref, ga_ref, gg_ref, x_ref, gt1_ref, sc2_ref, sh2_ref,
                  pa_ref, pg_ref, wo_ref, g1_ref, b1_ref, wrh_ref, wrl_ref, br_ref,
                  x1_ref, h2_ref, lt_ref):
    a = jnp.dot(ya_ref[...], pa_ref[...], preferred_element_type=F32)
    g = jnp.dot(yg_ref[...], pg_ref[...], preferred_element_type=F32)
    merged = (_sigmoid(ga_ref[...].astype(F32)) * a + _sigmoid(gg_ref[...].astype(F32)) * g)
    mix = jnp.dot(merged.astype(BF16), wo_ref[...], preferred_element_type=F32)
    x1 = _ln(DN_ALPHA * x_ref[...] + gt1_ref[...] * mix) * g1_ref[...] + b1_ref[...]
    x1_ref[...] = x1
    h2 = _ln(x1) * (1.0 + sc2_ref[...]) + sh2_ref[...]
    h2_ref[...] = h2
    h2_hi = h2.astype(BF16)
    h2_lo = (h2 - h2_hi.astype(F32)).astype(BF16)
    wrh = wrh_ref[...]
    logits = (lax.dot_general(wrh, h2_hi, NT_DIMS, preferred_element_type=F32)
              + lax.dot_general(wrh, h2_lo, NT_DIMS, preferred_element_type=F32)
              + lax.dot_general(wrl_ref[...], h2_hi, NT_DIMS, preferred_element_type=F32))
    lt_ref[...] = logits + br_ref[...]


def _merge(y_att, y_gla, proj, x2, ada3, p_a, p_g, w_o, g1, b1, wr_hi, wr_lo, b_r, seq,
           ga_blk, gg_blk):
    n, d = x2.shape
    tm = min(MERGE_TM, seq)
    per_b = seq // tm
    ne = wr_hi.shape[0]
    const = lambda shape: pl.BlockSpec(shape, lambda i: (0,) * len(shape),
                                       pipeline_mode=pl.Buffered(1))
    adarow = lambda k: pl.BlockSpec((None, 1, d), lambda i: ((i // per_b) * 6 + k, 0, 0))
    return pl.pallas_call(
        _merge_kernel,
        out_shape=(jax.ShapeDtypeStruct((n, d), F32),
                   jax.ShapeDtypeStruct((n, d), F32),
                   jax.ShapeDtypeStruct((ne, n), F32)),
        grid=(n // tm,),
        in_specs=[pl.BlockSpec((tm, y_att.shape[1]), lambda i: (i, 0)),
                  pl.BlockSpec((tm, d), lambda i: (i, 0)),
                  pl.BlockSpec((tm, d), lambda i: (i, ga_blk)),
                  pl.BlockSpec((tm, d), lambda i: (i, gg_blk)),
                  pl.BlockSpec((tm, d), lambda i: (i, 0)),
                  adarow(2), adarow(4), adarow(3),
                  const(p_a.shape), const(p_g.shape), const(w_o.shape),
                  const((1, d)), const((1, d)),
                  const(wr_hi.shape), const(wr_lo.shape), const((ne, 1))],
        out_specs=(pl.BlockSpec((tm, d), lambda i: (i, 0)),
                   pl.BlockSpec((tm, d), lambda i: (i, 0)),
                   pl.BlockSpec((ne, tm), lambda i: (0, i))),
        compiler_params=_cparams(("arbitrary",)),
        name="merge",
    )(y_att, y_gla, proj, proj, x2, ada3, ada3, ada3, p_a, p_g, w_o, g1, b1, wr_hi, wr_lo, b_r)


def _route_kernel(lt_ref, dest_ref, w_ref, blke_ref, bend_ref, oh_scr, rank_scr):
    ne, n = lt_ref.shape
    logits = lt_ref[...]
    eidx = lax.broadcasted_iota(I32, (ne, n), 0).astype(F32)
    vals, idxs = [], []
    for _ in range(TOP_K):
        m = jnp.max(logits, axis=0, keepdims=True)
        idx = jnp.min(jnp.where(logits == m, eidx, float(ne)), axis=0, keepdims=True)
        vals.append(m)
        idxs.append(idx)
        logits = jnp.where(eidx == idx, -jnp.inf, logits)
    exps = [jnp.exp(v - vals[0]) for v in vals]
    total = exps[0] + exps[1] + exps[2] + exps[3]
    for k in range(TOP_K):
        w_ref[k:k + 1, :] = exps[k] / total

    onehot = jnp.zeros((ne, n), F32)
    for k in range(TOP_K):
        onehot = onehot + (eidx == idxs[k]).astype(F32)
    oh_scr[...] = onehot.astype(BF16)

    r = lax.broadcasted_iota(I32, (LANES, 2 * LANES), 0)
    cidx = lax.broadcasted_iota(I32, (LANES, 2 * LANES), 1)
    scan_mat = ((r < cidx) | (cidx >= LANES)).astype(BF16)
    carry = jnp.zeros((ne, LANES), F32)
    for t in range(n // LANES):
        cols = slice(t * LANES, (t + 1) * LANES)
        both = jnp.dot(oh_scr[:, cols], scan_mat, preferred_element_type=F32)
        rank_scr[:, cols] = both[:, :LANES] + carry
        carry = carry + both[:, LANES:]

    counts = carry
    nblk = jnp.floor((counts + (MOE_BLOCK - 1)) * (1.0 / MOE_BLOCK))
    er = lax.broadcasted_iota(I32, (ne, ne), 0)
    ec = lax.broadcasted_iota(I32, (ne, ne), 1)
    strict_lower = (ec < er).astype(BF16)
    blk_start = jnp.dot(strict_lower, nblk.astype(BF16), preferred_element_type=F32)
    blk_end = blk_start + nblk
    row_start = blk_start[:, :1] * float(MOE_BLOCK)

    pos = rank_scr[...] + row_start
    for k in range(TOP_K):
        d = jnp.sum(jnp.where(eidx == idxs[k], pos, 0.0), axis=0, keepdims=True)
        dest_ref[k:k + 1, :] = d.astype(I32)

    nb_pad = blke_ref.shape[1]
    j = lax.broadcasted_iota(I32, (ne, nb_pad), 1).astype(F32)
    blke = jnp.sum((blk_end[:, :1] <= j).astype(F32), axis=0, keepdims=True)
    blke_ref[...] = jnp.minimum(blke, float(ne - 1)).astype(I32)
    bend_ref[...] = blk_end.astype(I32)


def _route(logits_t, n_blocks):
    ne, n = logits_t.shape
    nb_pad = -(-n_blocks // LANES) * LANES
    return pl.pallas_call(
        _route_kernel,
        out_shape=(jax.ShapeDtypeStruct((TOP_K, n), I32),
                   jax.ShapeDtypeStruct((TOP_K, n), F32),
                   jax.ShapeDtypeStruct((1, nb_pad), I32),
                   jax.ShapeDtypeStruct((ne, LANES), I32)),
        scratch_shapes=[pltpu.VMEM((ne, n), BF16), pltpu.VMEM((ne, n), F32)],
        compiler_params=pltpu.CompilerParams(vmem_limit_bytes=VMEM_LIMIT),
        name="route",
    )(logits_t)


def _invert_kernel(dest_ref, inv_ref):
    cap = inv_ref.shape[0]
    n_rows = dest_ref.shape[0]
    step = pl.program_id(0)
    half = pl.num_programs(0) // 2
    fill_per = cap // half
    scat_per = n_rows // half

    @pl.when(step < half)
    def _():
        def fill(g, carry):
            b = step * fill_per + g * ROW_UNROLL
            for u in range(ROW_UNROLL):
                inv_ref[b + u] = -1
            return carry
        lax.fori_loop(0, fill_per // ROW_UNROLL, fill, 0)

    @pl.when(step >= half)
    def _():
        def scatter(g, carry):
            b = (step - half) * scat_per + g * ROW_UNROLL
            for u in range(ROW_UNROLL):
                inv_ref[dest_ref[b + u]] = b + u
            return carry
        lax.fori_loop(0, scat_per // ROW_UNROLL, scatter, 0)


def _invert(dest_flat, cap):
    half = INVERT_STEPS
    assert cap % (half * ROW_UNROLL) == 0 and dest_flat.shape[0] % (half * ROW_UNROLL) == 0
    return pl.pallas_call(
        _invert_kernel,
        out_shape=jax.ShapeDtypeStruct((cap,), I32),
        grid=(2 * half,),
        in_specs=[pl.BlockSpec(memory_space=pltpu.SMEM)],
        out_specs=pl.BlockSpec(memory_space=pltpu.SMEM),
        compiler_params=pltpu.CompilerParams(dimension_semantics=("arbitrary",)),
        name="invert",
    )(dest_flat)


GU_CHUNK = 2 * LANES
EXPERT_RING = 5
ROW_DMA_GROUPS = 2


def _expert_kernel(blke_ref, bend_ref, inv_ref, h_ref, wgu_ref, wd_ref, bg_ref, bu_ref, bd_ref,
                   y4_ref, wg_scr, wu_scr, wd_scr, gu_stage, wd_stage, xbuf, ybuf,
                   sems, gsem, ssem, fence):
    j = pl.program_id(0)
    n_blocks = pl.num_programs(0)
    ne = bend_ref.shape[0]
    n_used = bend_ref[ne - 1]
    e = blke_ref[j]
    e_end = bend_ref[e]
    d, f = wg_scr.shape
    bm = xbuf.shape[1]
    n_tok = h_ref.shape[0]
    spare0 = TOP_K * n_tok
    units = f // LANES
    ring = gu_stage.shape[0]
    tf = min(EXPERT_TF, f)
    first = (j == 0) | (e != blke_ref[jnp.maximum(j - 1, 0)])
    last = j == e_end - 1
    has_next = e_end < n_used
    nxt = blke_ref[jnp.minimum(e_end, n_blocks - 1)]
    slot = j % 2
    nslot = 1 - slot

    def gather_row(blk, dslot, r):
        tok = lax.rem(jnp.maximum(inv_ref[blk * bm + r], 0), n_tok)
        pltpu.make_async_copy(h_ref.at[pl.ds(tok, 1), :], xbuf.at[dslot, pl.ds(r, 1), :],
                              gsem.at[dslot]).start()

    def issue_gather(blk, dslot):
        for r in range(bm):
            gather_row(blk, dslot, r)

    def wait_gather(dslot):
        pltpu.make_async_copy(h_ref.at[pl.ds(0, bm), :], xbuf.at[dslot], gsem.at[dslot]).wait()

    def scatter_row(blk, sslot, all_spare, r):
        flat = inv_ref[blk * bm + r]
        row = jnp.where(all_spare | (flat < 0), spare0 + sslot * bm + r, flat)
        pltpu.make_async_copy(ybuf.at[sslot, pl.ds(r, 1), :], y4_ref.at[pl.ds(row, 1), :],
                              ssem.at[sslot]).start()

    def issue_scatter(blk, sslot, all_spare):
        for r in range(bm):
            scatter_row(blk, sslot, all_spare, r)

    def wait_scatter(sslot):
        pltpu.make_async_copy(ybuf.at[sslot], y4_ref.at[pl.ds(0, bm), :], ssem.at[sslot]).wait()

    def gu_copy(ex, u, wslot):
        return pltpu.make_async_copy(wgu_ref.at[ex, :, pl.ds(u * GU_CHUNK, GU_CHUNK)],
                                     gu_stage.at[wslot], sems.at[0, wslot])

    def wd_copy(ex, u, wslot):
        return pltpu.make_async_copy(wd_ref.at[ex, pl.ds(u * LANES, LANES), :],
                                     wd_stage.at[wslot], sems.at[1, wslot])

    def start_unit(ex, u):
        gu_copy(ex, u, u % ring).start()
        wd_copy(ex, u, u % ring).start()

    def convert_unit(ex, u):
        wslot = u % ring
        gu_copy(ex, u, wslot).wait()
        wd_copy(ex, u, wslot).wait()
        src = lax.broadcasted_iota(I32, (GU_CHUNK, GU_CHUNK), 0)
        dst = lax.broadcasted_iota(I32, (GU_CHUNK, GU_CHUNK), 1)
        perm = (dst == (src >> 1) + (src & 1) * LANES).astype(BF16)
        cols = slice(u * LANES, (u + 1) * LANES)
        sep = jnp.dot(gu_stage[wslot].astype(BF16), perm, preferred_element_type=F32)
        wg_scr[:, cols] = sep[:, :LANES].astype(BF16)
        wu_scr[:, cols] = sep[:, LANES:].astype(BF16)
        wd_scr[cols, :] = wd_stage[wslot].astype(BF16)
        if u + ring < units:
            start_unit(ex, u + ring)

    @pl.when(j == 0)
    def _():
        ybuf[...] = jnp.zeros_like(ybuf)
        spare = pltpu.make_async_copy(ybuf.at[0], y4_ref.at[pl.ds(spare0, bm), :], ssem.at[0])
        spare.start()
        spare.wait()
        issue_gather(0, 0)
        for u in range(ring):
            start_unit(e, u)
        for u in range(units):
            convert_unit(e, u)

    @pl.when((j < n_used) & first & has_next)
    def _():
        for u in range(ring):
            start_unit(nxt, u)

    @pl.when((j >= 1) & (j < n_used))
    def _():
        wait_scatter(slot)

    def compute(replace):
        wait_gather(slot)
        x = xbuf[slot].astype(BF16)
        prev_blk = jnp.maximum(j - 1, 0)
        next_blk = jnp.minimum(j + 1, n_blocks - 1)
        jobs = ([lambda r=r: scatter_row(prev_blk, nslot, j == 0, r) for r in range(bm)]
                + [lambda r=r: gather_row(next_blk, nslot, r) for r in range(bm)])
        n_chunks = f // tf
        stride = max(1, n_chunks // ROW_DMA_GROUPS)
        per_point = -(-len(jobs) // (n_chunks // stride))

        def row_jobs():
            for _ in range(min(per_point, len(jobs))):
                jobs.pop(0)()
            pl.semaphore_signal(fence, 1)
            pl.semaphore_wait(fence, 1)

        acc = jnp.zeros((bm, d), F32)
        for c in range(f // tf):
            cols = slice(c * tf, (c + 1) * tf)
            g = jnp.dot(x, wg_scr[:, cols], preferred_element_type=F32) + bg_ref[:, cols]
            u = jnp.dot(x, wu_scr[:, cols], preferred_element_type=F32) + bu_ref[:, cols]
            gate = jnp.minimum(g, SWIGLU_LIMIT)
            up = jnp.clip(u, -SWIGLU_LIMIT, SWIGLU_LIMIT)
            act = (up + 1.0) * gate * _sigmoid(SWIGLU_ALPHA * gate)
            acc = acc + jnp.dot(act.astype(BF16), wd_scr[cols, :], preferred_element_type=F32)
            if (c + 1) % stride == 0:
                row_jobs()
            if replace:
                for unit in range(c * tf // LANES, (c + 1) * tf // LANES):
                    convert_unit(nxt, unit)
        ybuf[slot] = acc + bd_ref[...]

    @pl.when((j < n_used) & last & has_next)
    def _():
        compute(True)

    @pl.when((j < n_used) & jnp.logical_not(last & has_next))
    def _():
        compute(False)

    @pl.when(j == n_used - 1)
    def _():
        issue_scatter(j, slot, False)
        wait_scatter(slot)
        wait_scatter(nslot)
        wait_gather(nslot)


def _experts(blk_e, blk_end, inv, h2, w_gate_up, w_down, b_gate, b_up, b_down):
    n, d = h2.shape
    ne, f, _ = w_down.shape
    n_blocks = blk_e.shape[0]
    ring = min(EXPERT_RING, f // LANES)
    bspec = lambda width: pl.BlockSpec((None, 1, width), lambda j, be, bend, iv: (be[j], 0, 0))
    hbm = pl.BlockSpec(memory_space=pl.ANY)
    return pl.pallas_call(
        _expert_kernel,
        out_shape=jax.ShapeDtypeStruct((TOP_K * n + 2 * MOE_BLOCK, d), F32),
        grid_spec=pltpu.PrefetchScalarGridSpec(
            num_scalar_prefetch=3,
            grid=(n_blocks,),
            in_specs=[hbm, hbm, hbm, bspec(f), bspec(f), bspec(d)],
            out_specs=hbm,
            scratch_shapes=[pltpu.VMEM((d, f), BF16), pltpu.VMEM((d, f), BF16),
                            pltpu.VMEM((f, d), BF16),
                            pltpu.VMEM((ring, d, GU_CHUNK), F32), pltpu.VMEM((ring, LANES, d), F32),
                            pltpu.VMEM((2, MOE_BLOCK, d), F32), pltpu.VMEM((2, MOE_BLOCK, d), F32),
                            pltpu.SemaphoreType.DMA((2, ring)),
                            pltpu.SemaphoreType.DMA((2,)), pltpu.SemaphoreType.DMA((2,)),
                            pltpu.SemaphoreType.REGULAR]),
        compiler_params=_cparams(("arbitrary",)),
        name="experts",
    )(blk_e, blk_end, inv, h2, w_gate_up, w_down, b_gate, b_up, b_down)


def _combine_kernel(y0_ref, y1_ref, y2_ref, y3_ref, wt_ref, x1_ref, gt2_ref, g2_ref, b2_ref, o_ref):
    wt = wt_ref[...]
    ffn = y0_ref[...] * wt[:, 0:1]
    for k, y_ref in enumerate((y1_ref, y2_ref, y3_ref), start=1):
        ffn = ffn + y_ref[...] * wt[:, k:k + 1]
    z = DN_ALPHA * x1_ref[...] + gt2_ref[...] * ffn
    o_ref[...] = _ln(z) * g2_ref[...] + b2_ref[...]


def _combine(y4, w_t, x1, ada3, g2, b2, seq):
    n, d = x1.shape
    tm = min(ROW_TM, seq)
    per_b = seq // tm
    tiles = n // tm
    yspec = lambda k: pl.BlockSpec((tm, d), lambda i: (k * tiles + i, 0))
    return pl.pallas_call(
        _combine_kernel,
        out_shape=jax.ShapeDtypeStruct((n, d), F32),
        grid=(tiles,),
        in_specs=[yspec(0), yspec(1), yspec(2), yspec(3),
                  pl.BlockSpec((tm, TOP_K), lambda i: (i, 0)),
                  pl.BlockSpec((tm, d), lambda i: (i, 0)),
                  pl.BlockSpec((None, 1, d), lambda i: ((i // per_b) * 6 + 5, 0, 0)),
                  pl.BlockSpec((1, d), lambda i: (0, 0)),
                  pl.BlockSpec((1, d), lambda i: (0, 0))],
        out_specs=pl.BlockSpec((tm, d), lambda i: (i, 0)),
        compiler_params=_cparams(("arbitrary",)),
        name="combine",
    )(y4, y4, y4, y4, w_t, x1, ada3, g2, b2)


def kernel(x, c, w_ada, b_ada, w_in, w_gla_gate_up, b_gla_gate, attn_sinks, gla_norm_gain,
           w_branch_att, w_branch_gla, w_out, ln1_gain, ln1_bias, w_router, b_router,
           w_gate_up, b_gate_up, w_down, b_down, ln2_gain, ln2_bias):
    bsz, seq, d = x.shape
    n = bsz * seq
    qa_w = ATT_HEADS * ATT_HEAD_DIM
    kv_w = ATT_KV_HEADS * ATT_HEAD_DIM
    gk_w = d // 2
    splits = (qa_w, kv_w, kv_w, gk_w, gk_w, d, d, GLA_GATE_RANK, d, d)
    offs = [0]
    for wdt in splits:
        offs.append(offs[-1] + wdt)
    ne = w_router.shape[-1]
    f = w_down.shape[-2]

    x2 = x.reshape(n, d)
    for l in range(w_in.shape[0]):
        wl = w_in[l]
        part = lambda i: wl[:, offs[i]:offs[i + 1]]
        w_main = jnp.concatenate([part(3), part(4), part(5), part(6), part(8), part(9),
                                  part(0), part(1), part(2)], axis=1).astype(BF16)
        w_alr = jnp.pad(part(7), ((0, 0), (0, LANES - GLA_GATE_RANK))).astype(BF16)
        wg_pad = jnp.pad(w_gla_gate_up[l], ((0, LANES - GLA_GATE_RANK), (0, 0)))
        wr_t = w_router[l].T
        wr_hi = wr_t.astype(BF16)
        wr_lo = (wr_t - wr_hi.astype(F32)).astype(BF16)
        bgu = b_gate_up[l].reshape(ne, 1, f, 2)

        ada = _ada(c, w_ada[l], b_ada[l])
        ada3 = ada.reshape(bsz * 6, 1, d)

        proj, alr = _proj(x2, ada3, w_main, w_alr, seq)
        y_att = _swa(proj, attn_sinks[l], bsz, seq,
                     q_blk=(2 * gk_w + 4 * d) // qa_w,
                     k_blk=(2 * gk_w + 4 * d + qa_w) // kv_w,
                     v_blk=(2 * gk_w + 4 * d + qa_w + kv_w) // kv_w)
        dk = gk_w // GLA_HEADS
        dv = d // GLA_HEADS
        y_gla = _gla(proj, alr, wg_pad, b_gla_gate[l].reshape(1, gk_w),
                     gla_norm_gain[l].reshape(1, d), bsz, seq,
                     q_blk=0, k_blk=gk_w // dk, v_blk=2 * gk_w // dv, r_blk=(2 * gk_w + d) // dv)
        x1, h2, logits_t = _merge(
            y_att, y_gla, proj, x2, ada3,
            w_branch_att[l].astype(BF16), w_branch_gla[l].astype(BF16), w_out[l].astype(BF16),
            ln1_gain[l].reshape(1, d), ln1_bias[l].reshape(1, d),
            wr_hi, wr_lo, b_router[l].reshape(ne, 1), seq,
            ga_blk=(2 * gk_w + 2 * d) // d, gg_blk=(2 * gk_w + 3 * d) // d)

        n_rows = n * TOP_K
        cap = -(-n_rows // MOE_BLOCK) * MOE_BLOCK + ne * MOE_BLOCK
        n_blocks = cap // MOE_BLOCK
        dest, w_top, blk_e, blk_end = _route(logits_t, n_blocks)
        dest_flat = dest.reshape(-1)
        blk_e = blk_e.reshape(-1)[:n_blocks]
        blk_end = blk_end[:, 0]
        inv = _invert(dest_flat, cap)
        y4 = _experts(blk_e, blk_end, inv, h2, w_gate_up[l], w_down[l],
                      bgu[..., 0], bgu[..., 1], b_down[l].reshape(ne, 1, d))
        x2 = _combine(y4, w_top.T, x1, ada3,
                      ln2_gain[l].reshape(1, d), ln2_bias[l].reshape(1, d), seq)
    return x2.reshape(bsz, seq, d)
```

```python
import jax
import jax.numpy as jnp
from jax import lax
from jax.experimental import pallas as pl
from jax.experimental.pallas import tpu as pltpu

F32 = jnp.float32
BF16 = jnp.bfloat16
I32 = jnp.int32

ATT_HEADS = 16
ATT_KV_HEADS = 2
ATT_HEAD_DIM = 64
WINDOW = 128
GLA_HEADS = 4
GLA_GATE_RANK = 16
GLA_GATE_TEMP = 16.0
GLA_CHUNK = 64
N_EXPERTS = 32
TOP_K = 4
SWIGLU_LIMIT = 7.0
SWIGLU_ALPHA = 1.702
LN_EPS = 1e-5
DEPTH = 1
DN_ALPHA = (2 * DEPTH) ** 0.25

LANES = 128
VMEM_LIMIT = 56 * 1024 * 1024

MOE_BLOCK = 256
PROJ_TM = 1024
PROJ_TN = 1152
GLA_STEP = 256
GLA_HEADS_PER_STEP = 4
MERGE_TM = 256
ROW_TM = 256
ROW_UNROLL = 8
INVERT_STEPS = 32
EXPERT_TF = 512
LN_ROWS = 256

NT_DIMS = (((1,), (1,)), ((), ()))
TN_DIMS = (((0,), (0,)), ((), ()))


def _cparams(sem, vmem=VMEM_LIMIT):
    return pltpu.CompilerParams(dimension_semantics=sem, vmem_limit_bytes=vmem)


def _ln(x):
    mu = jnp.mean(x, axis=-1, keepdims=True)
    xc = x - mu
    var = jnp.mean(xc * xc, axis=-1, keepdims=True)
    return xc * lax.rsqrt(var + LN_EPS)


def _sigmoid(x):
    return 1.0 / (1.0 + jnp.exp(-x))


def _ada_kernel(c_ref, w_ref, b_ref, o_ref):
    c = c_ref[...]
    s = c * _sigmoid(c)
    o_ref[...] = jnp.dot(s.astype(BF16), w_ref[...].astype(BF16),
                         preferred_element_type=F32) + b_ref[...]


def _ada(c, w_ada, b_ada):
    bsz, d = c.shape
    n = w_ada.shape[1]
    tn = 1536
    return pl.pallas_call(
        _ada_kernel,
        out_shape=jax.ShapeDtypeStruct((bsz, n), F32),
        grid=(n // tn,),
        in_specs=[pl.BlockSpec((bsz, d), lambda j: (0, 0)),
                  pl.BlockSpec((d, tn), lambda j: (0, j)),
                  pl.BlockSpec((1, tn), lambda j: (0, j))],
        out_specs=pl.BlockSpec((bsz, tn), lambda j: (0, j)),
        compiler_params=_cparams(("arbitrary",)),
        name="ada",
    )(c, w_ada, b_ada.reshape(1, n))


def _proj_kernel(x_ref, sc_ref, sh_ref, w_ref, walr_ref, o_ref, alr_ref, h_scr):
    @pl.when(pl.program_id(1) == 0)
    def _():
        tm = x_ref.shape[0]
        sub = min(LN_ROWS, tm)
        for r0 in range(0, tm, sub):
            rows = slice(r0, r0 + sub)
            h = _ln(x_ref[rows, :]) * (1.0 + sc_ref[...]) + sh_ref[...]
            h_scr[rows, :] = h.astype(BF16)
        alr_ref[...] = jnp.dot(h_scr[...], walr_ref[...], preferred_element_type=F32)

    o_ref[...] = jnp.dot(h_scr[...], w_ref[...],
                         preferred_element_type=F32).astype(BF16)


def _proj(x2, ada3, w_main, w_alr, seq):
    n, d = x2.shape
    cols = w_main.shape[1]
    tm, tn = min(PROJ_TM, seq), PROJ_TN
    per_b = seq // tm
    return pl.pallas_call(
        _proj_kernel,
        out_shape=(jax.ShapeDtypeStruct((n, cols), BF16),
                   jax.ShapeDtypeStruct((n, LANES), F32)),
        grid=(n // tm, cols // tn),
        in_specs=[pl.BlockSpec((tm, d), lambda i, j: (i, 0)),
                  pl.BlockSpec((None, 1, d), lambda i, j: ((i // per_b) * 6 + 1, 0, 0)),
                  pl.BlockSpec((None, 1, d), lambda i, j: ((i // per_b) * 6 + 0, 0, 0)),
                  pl.BlockSpec((d, tn), lambda i, j: (0, j)),
                  pl.BlockSpec((d, LANES), lambda i, j: (0, 0))],
        out_specs=(pl.BlockSpec((tm, tn), lambda i, j: (i, j)),
                   pl.BlockSpec((tm, LANES), lambda i, j: (i, 0))),
        scratch_shapes=[pltpu.VMEM((tm, d), BF16)],
        compiler_params=_cparams(("arbitrary", "arbitrary")),
        name="proj",
    )(x2, ada3, ada3, w_main, w_alr)


def _swa_kernel(sink_ref, q_ref, kp_ref, kc_ref, vp_ref, vc_ref, o_ref):
    n = pl.program_id(1)
    blk = WINDOW
    half = ATT_HEAD_DIM
    group = ATT_HEADS // ATT_KV_HEADS

    lane = lax.broadcasted_iota(I32, (2 * blk, 2 * half), 1)
    lo = lane < half

    def variants(prev_ref, cur_ref, scale):
        band = jnp.concatenate([prev_ref[...], cur_ref[...]], axis=0).astype(F32) * scale
        rolled = pltpu.roll(band, half, axis=1)
        zero = jnp.zeros_like(band)
        return ((jnp.where(lo, band, zero).astype(BF16), jnp.where(lo, zero, rolled).astype(BF16)),
                (jnp.where(lo, rolled, zero).astype(BF16), jnp.where(lo, zero, band).astype(BF16)))

    kvar = variants(kp_ref, kc_ref, ATT_HEAD_DIM ** -0.5)
    vvar = variants(vp_ref, vc_ref, 1.0)

    qi = lax.broadcasted_iota(I32, (blk, 2 * blk), 0)
    si = lax.broadcasted_iota(I32, (blk, 2 * blk), 1)
    dist = qi + blk - si
    first_key = jnp.where(n > 0, 0, blk)
    valid = (dist >= 0) & (dist < WINDOW) & (si >= first_key)
    dist_f = dist.astype(F32)

    for pair in range(ATT_HEADS // 2):
        q_pair = q_ref[:, pair * 2 * half:(pair + 1) * 2 * half]
        acc = jnp.zeros((blk, 2 * half), F32)
        for sub in range(2):
            h = pair * 2 + sub
            kh = h // group
            slope = 2.0 ** (-8.0 * (h + 1.0) / ATT_HEADS)
            s = lax.dot_general(q_pair, kvar[kh][sub], NT_DIMS, preferred_element_type=F32)
            s = jnp.where(valid, s - slope * dist_f, -jnp.inf)
            sink = sink_ref[h]
            m = jnp.maximum(jnp.max(s, axis=-1, keepdims=True), sink)
            p = jnp.exp(s - m)
            denom = jnp.sum(p, axis=-1, keepdims=True) + jnp.exp(sink - m)
            p = p / denom
            acc = acc + jnp.dot(p.astype(BF16), vvar[kh][sub], preferred_element_type=F32)
        o_ref[:, pair * 2 * half:(pair + 1) * 2 * half] = acc.astype(BF16)


def _swa(proj, sinks, bsz, seq, q_blk, k_blk, v_blk):
    n = bsz * seq
    nb = seq // WINDOW
    qw = ATT_HEADS * ATT_HEAD_DIM
    kw = ATT_KV_HEADS * ATT_HEAD_DIM
    cur = lambda col: (lambda b, i: (b * nb + i, col))
    prev = lambda col: (lambda b, i: (b * nb + jnp.maximum(i - 1, 0), col))
    return pl.pallas_call(
        _swa_kernel,
        out_shape=jax.ShapeDtypeStruct((n, qw), BF16),
        grid=(bsz, nb),
        in_specs=[pl.BlockSpec(memory_space=pltpu.SMEM),
                  pl.BlockSpec((WINDOW, qw), cur(q_blk)),
                  pl.BlockSpec((WINDOW, kw), prev(k_blk)),
                  pl.BlockSpec((WINDOW, kw), cur(k_blk)),
                  pl.BlockSpec((WINDOW, kw), prev(v_blk)),
                  pl.BlockSpec((WINDOW, kw), cur(v_blk))],
        out_specs=pl.BlockSpec((WINDOW, qw), lambda b, i: (b * nb + i, 0)),
        compiler_params=_cparams(("arbitrary", "arbitrary")),
        name="swa",
    )(sinks, proj, proj, proj, proj, proj)


def _gla_kernel(q_ref, k_ref, v_ref, r_ref, alr_ref, wg_ref, bg_ref, gain_ref, o_ref, st_scr):
    c = GLA_CHUNK
    hp = st_scr.shape[0]
    ts = q_ref.shape[0]
    dk = q_ref.shape[1] // hp
    dv = v_ref.shape[1] // hp
    nc = ts // c

    @pl.when(pl.program_id(2) == 0)
    def _():
        st_scr[...] = jnp.zeros_like(st_scr)

    z = jnp.dot(alr_ref[...], wg_ref[...], preferred_element_type=F32,
                precision=lax.Precision.HIGHEST) + bg_ref[...]
    log_a = (jnp.minimum(z, 0.0) - jnp.log1p(jnp.exp(-jnp.abs(z)))) / GLA_GATE_TEMP
    row = lax.broadcasted_iota(I32, (ts, ts), 0)
    col = lax.broadcasted_iota(I32, (ts, ts), 1)
    causal = (row >= col) & ((row // c) == (col // c))
    b = jnp.dot(causal.astype(F32), log_a, preferred_element_type=F32,
                precision=lax.Precision.HIGHEST)
    b_last = [b[(ci + 1) * c - 1:(ci + 1) * c, :] for ci in range(nc)]
    b_end = jnp.concatenate([jnp.broadcast_to(bl, (c, hp * dk)) for bl in b_last], axis=0)
    q = q_ref[...].astype(F32)
    k = k_ref[...].astype(F32)
    q_e = (q * jnp.exp(b) * (dk ** -0.5)).astype(BF16)
    k_e = (k * jnp.exp(-b)).astype(BF16)
    k_end = (k * jnp.exp(b_end - b)).astype(BF16)
    decay = [jnp.exp(bl) for bl in b_last]

    outs = []
    for h in range(hp):
        kc = slice(h * dk, (h + 1) * dk)
        v = v_ref[:, h * dv:(h + 1) * dv]
        a = lax.dot_general(q_e[:, kc], k_e[:, kc], NT_DIMS, preferred_element_type=F32)
        a = jnp.where(causal, a, 0.0).astype(BF16)
        o_intra = jnp.dot(a, v, preferred_element_type=F32)
        st = st_scr[h]
        o_inter = []
        for ci in range(nc):
            rows = slice(ci * c, (ci + 1) * c)
            o_inter.append(lax.dot_general(q_e[rows, kc], st.astype(BF16), NT_DIMS,
                                           preferred_element_type=F32))
            upd = lax.dot_general(v[rows, :], k_end[rows, kc], TN_DIMS,
                                  preferred_element_type=F32)
            st = st * decay[ci][:, kc] + upd
        st_scr[h] = st
        o = o_intra + jnp.concatenate(o_inter, axis=0)
        outs.append(o * lax.rsqrt(jnp.mean(o * o, axis=-1, keepdims=True) + LN_EPS))

    o = jnp.concatenate(outs, axis=1) * gain_ref[...]
    r = r_ref[...].astype(F32)
    o_ref[...] = (o * (r * _sigmoid(r))).astype(BF16)


def _gla(proj, alr, wg_pad, bg, gain, bsz, seq, q_blk, k_blk, v_blk, r_blk):
    n = bsz * seq
    hp = GLA_HEADS_PER_STEP
    dk = wg_pad.shape[1] // GLA_HEADS
    dv = gain.shape[1] // GLA_HEADS
    ts = min(GLA_STEP, seq)
    steps = seq // ts
    assert q_blk % hp == 0 and k_blk % hp == 0 and v_blk % hp == 0 and r_blk % hp == 0
    rowmap = lambda col0: (lambda b, h, t: (b * steps + t, col0 // hp + h))
    return pl.pallas_call(
        _gla_kernel,
        out_shape=jax.ShapeDtypeStruct((n, GLA_HEADS * dv), BF16),
        grid=(bsz, GLA_HEADS // hp, steps),
        in_specs=[pl.BlockSpec((ts, hp * dk), rowmap(q_blk)),
                  pl.BlockSpec((ts, hp * dk), rowmap(k_blk)),
                  pl.BlockSpec((ts, hp * dv), rowmap(v_blk)),
                  pl.BlockSpec((ts, hp * dv), rowmap(r_blk)),
                  pl.BlockSpec((ts, LANES), lambda b, h, t: (b * steps + t, 0)),
                  pl.BlockSpec((LANES, hp * dk), lambda b, h, t: (0, h)),
                  pl.BlockSpec((1, hp * dk), lambda b, h, t: (0, h)),
                  pl.BlockSpec((1, hp * dv), lambda b, h, t: (0, h))],
        out_specs=pl.BlockSpec((ts, hp * dv), rowmap(0)),
        scratch_shapes=[pltpu.VMEM((hp, dv, dk), F32)],
        compiler_params=_cparams(("arbitrary", "arbitrary", "arbitrary")),
        name="gla",
    )(proj, proj, proj, proj, alr, wg_pad, bg, gain)


def _merge_kernel(ya_ref, yg_ref, ga_ref, gg_ref, x_ref, gt1_ref, sc2_ref, sh2_ref,
                  pa_ref, pg_ref, wo_ref, g1_ref, b1_ref, wrh_ref, wrl_ref, br_ref,
                  x1_ref, h2_ref, lt_ref):
    a = jnp.dot(ya_ref[...], pa_ref[...], preferred_element_type=F32)
    g = jnp.dot(yg_ref[...], pg_ref[...], preferred_element_type=F32)
    merged = (_sigmoid(ga_ref[...].astype(F32)) * a + _sigmoid(gg_ref[...].astype(F32)) * g)
    mix = jnp.dot(merged.astype(BF16), wo_ref[...], preferred_element_type=F32)
    x1 = _ln(DN_ALPHA * x_ref[...] + gt1_ref[...] * mix) * g1_ref[...] + b1_ref[...]
    x1_ref[...] = x1
    h2 = _ln(x1) * (1.0 + sc2_ref[...]) + sh2_ref[...]
    h2_ref[...] = h2
    h2_hi = h2.astype(BF16)
    h2_lo = (h2 - h2_hi.astype(F32)).astype(BF16)
    wrh = wrh_ref[...]
    logits = (lax.dot_general(wrh, h2_hi, NT_DIMS, preferred_element_type=F32)
              + lax.dot_general(wrh, h2_lo, NT_DIMS, preferred_element_type=F32)
              + lax.dot_general(wrl_ref[...], h2_hi, NT_DIMS, preferred_element_type=F32))
    lt_ref[...] = logits + br_ref[...]


def _merge(y_att, y_gla, proj, x2, ada3, p_a, p_g, w_o, g1, b1, wr_hi, wr_lo, b_r, seq,
           ga_blk, gg_blk):
    n, d = x2.shape
    tm = min(MERGE_TM, seq)
    per_b = seq // tm
    ne = wr_hi.shape[0]
    const = lambda shape: pl.BlockSpec(shape, lambda i: (0,) * len(shape),
                                       pipeline_mode=pl.Buffered(1))
    adarow = lambda k: pl.BlockSpec((None, 1, d), lambda i: ((i // per_b) * 6 + k, 0, 0))
    return pl.pallas_call(
        _merge_kernel,
        out_shape=(jax.ShapeDtypeStruct((n, d), F32),
                   jax.ShapeDtypeStruct((n, d), F32),
                   jax.ShapeDtypeStruct((ne, n), F32)),
        grid=(n // tm,),
        in_specs=[pl.BlockSpec((tm, y_att.shape[1]), lambda i: (i, 0)),
                  pl.BlockSpec((tm, d), lambda i: (i, 0)),
                  pl.BlockSpec((tm, d), lambda i: (i, ga_blk)),
                  pl.BlockSpec((tm, d), lambda i: (i, gg_blk)),
                  pl.BlockSpec((tm, d), lambda i: (i, 0)),
                  adarow(2), adarow(4), adarow(3),
                  const(p_a.shape), const(p_g.shape), const(w_o.shape),
                  const((1, d)), const((1, d)),
                  const(wr_hi.shape), const(wr_lo.shape), const((ne, 1))],
        out_specs=(pl.BlockSpec((tm, d), lambda i: (i, 0)),
                   pl.BlockSpec((tm, d), lambda i: (i, 0)),
                   pl.BlockSpec((ne, tm), lambda i: (0, i))),
        compiler_params=_cparams(("arbitrary",)),
        name="merge",
    )(y_att, y_gla, proj, proj, x2, ada3, ada3, ada3, p_a, p_g, w_o, g1, b1, wr_hi, wr_lo, b_r)


def _route_kernel(lt_ref, dest_ref, w_ref, blke_ref, bend_ref, oh_scr, rank_scr):
    ne, n = lt_ref.shape
    logits = lt_ref[...]
    eidx = lax.broadcasted_iota(I32, (ne, n), 0).astype(F32)
    vals, idxs = [], []
    for _ in range(TOP_K):
        m = jnp.max(logits, axis=0, keepdims=True)
        idx = jnp.min(jnp.where(logits == m, eidx, float(ne)), axis=0, keepdims=True)
        vals.append(m)
        idxs.append(idx)
        logits = jnp.where(eidx == idx, -jnp.inf, logits)
    exps = [jnp.exp(v - vals[0]) for v in vals]
    total = exps[0] + exps[1] + exps[2] + exps[3]
    for k in range(TOP_K):
        w_ref[k:k + 1, :] = exps[k] / total

    onehot = jnp.zeros((ne, n), F32)
    for k in range(TOP_K):
        onehot = onehot + (eidx == idxs[k]).astype(F32)
    oh_scr[...] = onehot.astype(BF16)

    r = lax.broadcasted_iota(I32, (LANES, 2 * LANES), 0)
    cidx = lax.broadcasted_iota(I32, (LANES, 2 * LANES), 1)
    scan_mat = ((r < cidx) | (cidx >= LANES)).astype(BF16)
    carry = jnp.zeros((ne, LANES), F32)
    for t in range(n // LANES):
        cols = slice(t * LANES, (t + 1) * LANES)
        both = jnp.dot(oh_scr[:, cols], scan_mat, preferred_element_type=F32)
        rank_scr[:, cols] = both[:, :LANES] + carry
        carry = carry + both[:, LANES:]

    counts = carry
    nblk = jnp.floor((counts + (MOE_BLOCK - 1)) * (1.0 / MOE_BLOCK))
    er = lax.broadcasted_iota(I32, (ne, ne), 0)
    ec = lax.broadcasted_iota(I32, (ne, ne), 1)
    strict_lower = (ec < er).astype(BF16)
    blk_start = jnp.dot(strict_lower, nblk.astype(BF16), preferred_element_type=F32)
    blk_end = blk_start + nblk
    row_start = blk_start[:, :1] * float(MOE_BLOCK)

    pos = rank_scr[...] + row_start
    for k in range(TOP_K):
        d = jnp.sum(jnp.where(eidx == idxs[k], pos, 0.0), axis=0, keepdims=True)
        dest_ref[k:k + 1, :] = d.astype(I32)

    nb_pad = blke_ref.shape[1]
    j = lax.broadcasted_iota(I32, (ne, nb_pad), 1).astype(F32)
    blke = jnp.sum((blk_end[:, :1] <= j).astype(F32), axis=0, keepdims=True)
    blke_ref[...] = jnp.minimum(blke, float(ne - 1)).astype(I32)
    bend_ref[...] = blk_end.astype(I32)


def _route(logits_t, n_blocks):
    ne, n = logits_t.shape
    nb_pad = -(-n_blocks // LANES) * LANES
    return pl.pallas_call(
        _route_kernel,
        out_shape=(jax.ShapeDtypeStruct((TOP_K, n), I32),
                   jax.ShapeDtypeStruct((TOP_K, n), F32),
                   jax.ShapeDtypeStruct((1, nb_pad), I32),
                   jax.ShapeDtypeStruct((ne, LANES), I32)),
        scratch_shapes=[pltpu.VMEM((ne, n), BF16), pltpu.VMEM((ne, n), F32)],
        compiler_params=pltpu.CompilerParams(vmem_limit_bytes=VMEM_LIMIT),
        name="route",
    )(logits_t)


def _invert_kernel(dest_ref, inv_ref):
    cap = inv_ref.shape[0]
    n_rows = dest_ref.shape[0]
    step = pl.program_id(0)
    half = pl.num_programs(0) // 2
    fill_per = cap // half
    scat_per = n_rows // half

    @pl.when(step < half)
    def _():
        def fill(g, carry):
            b = step * fill_per + g * ROW_UNROLL
            for u in range(ROW_UNROLL):
                inv_ref[b + u] = -1
            return carry
        lax.fori_loop(0, fill_per // ROW_UNROLL, fill, 0)

    @pl.when(step >= half)
    def _():
        def scatter(g, carry):
            b = (step - half) * scat_per + g * ROW_UNROLL
            for u in range(ROW_UNROLL):
                inv_ref[dest_ref[b + u]] = b + u
            return carry
        lax.fori_loop(0, scat_per // ROW_UNROLL, scatter, 0)


def _invert(dest_flat, cap):
    half = INVERT_STEPS
    assert cap % (half * ROW_UNROLL) == 0 and dest_flat.shape[0] % (half * ROW_UNROLL) == 0
    return pl.pallas_call(
        _invert_kernel,
        out_shape=jax.ShapeDtypeStruct((cap,), I32),
        grid=(2 * half,),
        in_specs=[pl.BlockSpec(memory_space=pltpu.SMEM)],
        out_specs=pl.BlockSpec(memory_space=pltpu.SMEM),
        compiler_params=pltpu.CompilerParams(dimension_semantics=("arbitrary",)),
        name="invert",
    )(dest_flat)


GU_CHUNK = 2 * LANES
EXPERT_RING = 5
ROW_DMA_AFTER = (1, 3)


def _expert_kernel(blke_ref, bend_ref, inv_ref, h_ref, wgu_ref, wd_ref, bg_ref, bu_ref, bd_ref,
                   y4_ref, wg_scr, wu_scr, wd_scr, gu_stage, wd_stage, xbuf, ybuf,
                   sems, gsem, ssem, fence):
    j = pl.program_id(0)
    n_blocks = pl.num_programs(0)
    ne = bend_ref.shape[0]
    n_used = bend_ref[ne - 1]
    e = blke_ref[j]
    e_end = bend_ref[e]
    d, f = wg_scr.shape
    bm = xbuf.shape[1]
    n_tok = h_ref.shape[0]
    spare0 = TOP_K * n_tok
    units = f // LANES
    ring = gu_stage.shape[0]
    tf = min(EXPERT_TF, f)
    first = (j == 0) | (e != blke_ref[jnp.maximum(j - 1, 0)])
    last = j == e_end - 1
    has_next = e_end < n_used
    nxt = blke_ref[jnp.minimum(e_end, n_blocks - 1)]
    slot = j % 2
    nslot = 1 - slot

    def gather_row(blk, dslot, r):
        tok = lax.rem(jnp.maximum(inv_ref[blk * bm + r], 0), n_tok)
        pltpu.make_async_copy(h_ref.at[pl.ds(tok, 1), :], xbuf.at[dslot, pl.ds(r, 1), :],
                              gsem.at[dslot]).start()

    def issue_gather(blk, dslot):
        for r in range(bm):
            gather_row(blk, dslot, r)

    def wait_gather(dslot):
        pltpu.make_async_copy(h_ref.at[pl.ds(0, bm), :], xbuf.at[dslot], gsem.at[dslot]).wait()

    def scatter_row(blk, sslot, all_spare, r):
        flat = inv_ref[blk * bm + r]
        row = jnp.where(all_spare | (flat < 0), spare0 + sslot * bm + r, flat)
        pltpu.make_async_copy(ybuf.at[sslot, pl.ds(r, 1), :], y4_ref.at[pl.ds(row, 1), :],
                              ssem.at[sslot]).start()

    def issue_scatter(blk, sslot, all_spare):
        for r in range(bm):
            scatter_row(blk, sslot, all_spare, r)

    def wait_scatter(sslot):
        pltpu.make_async_copy(ybuf.at[sslot], y4_ref.at[pl.ds(0, bm), :], ssem.at[sslot]).wait()

    def gu_copy(ex, u, wslot):
        return pltpu.make_async_copy(wgu_ref.at[ex, :, pl.ds(u * GU_CHUNK, GU_CHUNK)],
                                     gu_stage.at[wslot], sems.at[0, wslot])

    def wd_copy(ex, u, wslot):
        return pltpu.make_async_copy(wd_ref.at[ex, pl.ds(u * LANES, LANES), :],
                                     wd_stage.at[wslot], sems.at[1, wslot])

    def start_unit(ex, u):
        gu_copy(ex, u, u % ring).start()
        wd_copy(ex, u, u % ring).start()

    def convert_unit(ex, u):
        wslot = u % ring
        gu_copy(ex, u, wslot).wait()
        wd_copy(ex, u, wslot).wait()
        src = lax.broadcasted_iota(I32, (GU_CHUNK, GU_CHUNK), 0)
        dst = lax.broadcasted_iota(I32, (GU_CHUNK, GU_CHUNK), 1)
        perm = (dst == (src >> 1) + (src & 1) * LANES).astype(BF16)
        cols = slice(u * LANES, (u + 1) * LANES)
        sep = jnp.dot(gu_stage[wslot].astype(BF16), perm, preferred_element_type=F32)
        wg_scr[:, cols] = sep[:, :LANES].astype(BF16)
        wu_scr[:, cols] = sep[:, LANES:].astype(BF16)
        wd_scr[cols, :] = wd_stage[wslot].astype(BF16)
        if u + ring < units:
            start_unit(ex, u + ring)

    @pl.when(j == 0)
    def _():
        ybuf[...] = jnp.zeros_like(ybuf)
        pltpu.make_async_copy(ybuf.at[0], y4_ref.at[pl.ds(spare0, bm), :], ssem.at[0]).start()
        issue_gather(0, 0)
        for u in range(ring):
            start_unit(e, u)
        for u in range(units):
            convert_unit(e, u)

    @pl.when((j < n_used) & first & has_next)
    def _():
        for u in range(ring):
            start_unit(nxt, u)

    def compute(replace):
        wait_gather(slot)
        x = xbuf[slot].astype(BF16)
        prev_blk = jnp.maximum(j - 1, 0)
        next_blk = jnp.minimum(j + 1, n_blocks - 1)
        jobs = ([lambda r=r: gather_row(next_blk, nslot, r) for r in range(bm)]
                + [lambda r=r: scatter_row(prev_blk, nslot, j == 0, r) for r in range(bm)])
        n_chunks = f // tf
        after = ROW_DMA_AFTER if n_chunks > max(ROW_DMA_AFTER) else (n_chunks - 1,)
        per_point = -(-len(jobs) // len(after))

        def row_jobs():
            for _ in range(min(per_point, len(jobs))):
                jobs.pop(0)()
            pl.semaphore_signal(fence, 1)
            pl.semaphore_wait(fence, 1)

        acc = jnp.zeros((bm, d), F32)
        for c in range(f // tf):
            cols = slice(c * tf, (c + 1) * tf)
            g = jnp.dot(x, wg_scr[:, cols], preferred_element_type=F32) + bg_ref[:, cols]
            u = jnp.dot(x, wu_scr[:, cols], preferred_element_type=F32) + bu_ref[:, cols]
            gate = jnp.minimum(g, SWIGLU_LIMIT)
            up = jnp.clip(u, -SWIGLU_LIMIT, SWIGLU_LIMIT)
            act = (up + 1.0) * gate * _sigmoid(SWIGLU_ALPHA * gate)
            acc = acc + jnp.dot(act.astype(BF16), wd_scr[cols, :], preferred_element_type=F32)
            if c in after:
                row_jobs()
            if replace:
                for unit in range(c * tf // LANES, (c + 1) * tf // LANES):
                    convert_unit(nxt, unit)
        wait_scatter(slot)
        ybuf[slot] = acc + bd_ref[...]

    @pl.when((j < n_used) & last & has_next)
    def _():
        compute(True)

    @pl.when((j < n_used) & jnp.logical_not(last & has_next))
    def _():
        compute(False)

    @pl.when(j == n_used - 1)
    def _():
        issue_scatter(j, slot, False)
        wait_scatter(slot)
        wait_scatter(nslot)
        wait_gather(nslot)


def _experts(blk_e, blk_end, inv, h2, w_gate_up, w_down, b_gate, b_up, b_down):
    n, d = h2.shape
    ne, f, _ = w_down.shape
    n_blocks = blk_e.shape[0]
    ring = min(EXPERT_RING, f // LANES)
    bspec = lambda width: pl.BlockSpec((None, 1, width), lambda j, be, bend, iv: (be[j], 0, 0))
    hbm = pl.BlockSpec(memory_space=pl.ANY)
    return pl.pallas_call(
        _expert_kernel,
        out_shape=jax.ShapeDtypeStruct((TOP_K * n + 2 * MOE_BLOCK, d), F32),
        grid_spec=pltpu.PrefetchScalarGridSpec(
            num_scalar_prefetch=3,
            grid=(n_blocks,),
            in_specs=[hbm, hbm, hbm, bspec(f), bspec(f), bspec(d)],
            out_specs=hbm,
            scratch_shapes=[pltpu.VMEM((d, f), BF16), pltpu.VMEM((d, f), BF16),
                            pltpu.VMEM((f, d), BF16),
                            pltpu.VMEM((ring, d, GU_CHUNK), F32), pltpu.VMEM((ring, LANES, d), F32),
                            pltpu.VMEM((2, MOE_BLOCK, d), F32), pltpu.VMEM((2, MOE_BLOCK, d), F32),
                            pltpu.SemaphoreType.DMA((2, ring)),
                            pltpu.SemaphoreType.DMA((2,)), pltpu.SemaphoreType.DMA((2,)),
                            pltpu.SemaphoreType.REGULAR]),
        compiler_params=_cparams(("arbitrary",)),
        name="experts",
    )(blk_e, blk_end, inv, h2, w_gate_up, w_down, b_gate, b_up, b_down)


def _combine_kernel(y0_ref, y1_ref, y2_ref, y3_ref, wt_ref, x1_ref, gt2_ref, g2_ref, b2_ref, o_ref):
    wt = wt_ref[...]
    ffn = y0_ref[...] * wt[:, 0:1]
    for k, y_ref in enumerate((y1_ref, y2_ref, y3_ref), start=1):
        ffn = ffn + y_ref[...] * wt[:, k:k + 1]
    z = DN_ALPHA * x1_ref[...] + gt2_ref[...] * ffn
    o_ref[...] = _ln(z) * g2_ref[...] + b2_ref[...]


def _combine(y4, w_t, x1, ada3, g2, b2, seq):
    n, d = x1.shape
    tm = min(ROW_TM, seq)
    per_b = seq // tm
    tiles = n // tm
    yspec = lambda k: pl.BlockSpec((tm, d), lambda i: (k * tiles + i, 0))
    return pl.pallas_call(
        _combine_kernel,
        out_shape=jax.ShapeDtypeStruct((n, d), F32),
        grid=(tiles,),
        in_specs=[yspec(0), yspec(1), yspec(2), yspec(3),
                  pl.BlockSpec((tm, TOP_K), lambda i: (i, 0)),
                  pl.BlockSpec((tm, d), lambda i: (i, 0)),
                  pl.BlockSpec((None, 1, d), lambda i: ((i // per_b) * 6 + 5, 0, 0)),
                  pl.BlockSpec((1, d), lambda i: (0, 0)),
                  pl.BlockSpec((1, d), lambda i: (0, 0))],
        out_specs=pl.BlockSpec((tm, d), lambda i: (i, 0)),
        compiler_params=_cparams(("arbitrary",)),
        name="combine",
    )(y4, y4, y4, y4, w_t, x1, ada3, g2, b2)


def kernel(x, c, w_ada, b_ada, w_in, w_gla_gate_up, b_gla_gate, attn_sinks, gla_norm_gain,
           w_branch_att, w_branch_gla, w_out, ln1_gain, ln1_bias, w_router, b_router,
           w_gate_up, b_gate_up, w_down, b_down, ln2_gain, ln2_bias):
    bsz, seq, d = x.shape
    n = bsz * seq
    qa_w = ATT_HEADS * ATT_HEAD_DIM
    kv_w = ATT_KV_HEADS * ATT_HEAD_DIM
    gk_w = d // 2
    splits = (qa_w, kv_w, kv_w, gk_w, gk_w, d, d, GLA_GATE_RANK, d, d)
    offs = [0]
    for wdt in splits:
        offs.append(offs[-1] + wdt)
    ne = w_router.shape[-1]
    f = w_down.shape[-2]

    x2 = x.reshape(n, d)
    for l in range(w_in.shape[0]):
        wl = w_in[l]
        part = lambda i: wl[:, offs[i]:offs[i + 1]]
        w_main = jnp.concatenate([part(3), part(4), part(5), part(6), part(8), part(9),
                                  part(0), part(1), part(2)], axis=1).astype(BF16)
        w_alr = jnp.pad(part(7), ((0, 0), (0, LANES - GLA_GATE_RANK))).astype(BF16)
        wg_pad = jnp.pad(w_gla_gate_up[l], ((0, LANES - GLA_GATE_RANK), (0, 0)))
        wr_t = w_router[l].T
        wr_hi = wr_t.astype(BF16)
        wr_lo = (wr_t - wr_hi.astype(F32)).astype(BF16)
        bgu = b_gate_up[l].reshape(ne, 1, f, 2)

        ada = _ada(c, w_ada[l], b_ada[l])
        ada3 = ada.reshape(bsz * 6, 1, d)

        proj, alr = _proj(x2, ada3, w_main, w_alr, seq)
        y_att = _swa(proj, attn_sinks[l], bsz, seq,
                     q_blk=(2 * gk_w + 4 * d) // qa_w,
                     k_blk=(2 * gk_w + 4 * d + qa_w) // kv_w,
                     v_blk=(2 * gk_w + 4 * d + qa_w + kv_w) // kv_w)
        dk = gk_w // GLA_HEADS
        dv = d // GLA_HEADS
        y_gla = _gla(proj, alr, wg_pad, b_gla_gate[l].reshape(1, gk_w),
                     gla_norm_gain[l].reshape(1, d), bsz, seq,
                     q_blk=0, k_blk=gk_w // dk, v_blk=2 * gk_w // dv, r_blk=(2 * gk_w + d) // dv)
        x1, h2, logits_t = _merge(
            y_att, y_gla, proj, x2, ada3,
            w_branch_att[l].astype(BF16), w_branch_gla[l].astype(BF16), w_out[l].astype(BF16),
            ln1_gain[l].reshape(1, d), ln1_bias[l].reshape(1, d),
            wr_hi, wr_lo, b_router[l].reshape(ne, 1), seq,
            ga_blk=(2 * gk_w + 2 * d) // d, gg_blk=(2 * gk_w + 3 * d) // d)

        n_rows = n * TOP_K
        cap = -(-n_rows // MOE_BLOCK) * MOE_BLOCK + ne * MOE_BLOCK
        n_blocks = cap // MOE_BLOCK
        dest, w_top, blk_e, blk_end = _route(logits_t, n_blocks)
        dest_flat = dest.reshape(-1)
        blk_e = blk_e.reshape(-1)[:n_blocks]
        blk_end = blk_end[:, 0]
        inv = _invert(dest_flat, cap)
        y4 = _experts(blk_e, blk_end, inv, h2, w_gate_up[l], w_down[l],
                      bgu[..., 0], bgu[..., 1], b_down[l].reshape(ne, 1, d))
        x2 = _combine(y4, w_top.T, x1, ada3,
                      ln2_gain[l].reshape(1, d), ln2_bias[l].reshape(1, d), seq)
    return x2.reshape(bsz, seq, d)
```

```python
import jax
import jax.numpy as jnp
from jax import lax
from jax.experimental import pallas as pl
from jax.experimental.pallas import tpu as pltpu

F32 = jnp.float32
BF16 = jnp.bfloat16
I32 = jnp.int32

ATT_HEADS = 16
ATT_KV_HEADS = 2
ATT_HEAD_DIM = 64
WINDOW = 128
GLA_HEADS = 4
GLA_GATE_RANK = 16
GLA_GATE_TEMP = 16.0
GLA_CHUNK = 64
N_EXPERTS = 32
TOP_K = 4
SWIGLU_LIMIT = 7.0
SWIGLU_ALPHA = 1.702
LN_EPS = 1e-5
DEPTH = 1
DN_ALPHA = (2 * DEPTH) ** 0.25

LANES = 128
VMEM_LIMIT = 56 * 1024 * 1024

MOE_BLOCK = 256
PROJ_TM = 1024
PROJ_TN = 1152
GLA_STEP = 256
GLA_HEADS_PER_STEP = 4
MERGE_TM = 256
ROW_TM = 256
ROW_UNROLL = 8
INVERT_STEPS = 32
EXPERT_TF = 1024
LN_ROWS = 256

NT_DIMS = (((1,), (1,)), ((), ()))
TN_DIMS = (((0,), (0,)), ((), ()))


def _cparams(sem, vmem=VMEM_LIMIT):
    return pltpu.CompilerParams(dimension_semantics=sem, vmem_limit_bytes=vmem)


def _ln(x):
    mu = jnp.mean(x, axis=-1, keepdims=True)
    xc = x - mu
    var = jnp.mean(xc * xc, axis=-1, keepdims=True)
    return xc * lax.rsqrt(var + LN_EPS)


def _sigmoid(x):
    return 1.0 / (1.0 + jnp.exp(-x))


def _ada_kernel(c_ref, w_ref, b_ref, o_ref):
    c = c_ref[...]
    s = c * _sigmoid(c)
    o_ref[...] = jnp.dot(s.astype(BF16), w_ref[...].astype(BF16),
                         preferred_element_type=F32) + b_ref[...]


def _ada(c, w_ada, b_ada):
    bsz, d = c.shape
    n = w_ada.shape[1]
    tn = 1536
    return pl.pallas_call(
        _ada_kernel,
        out_shape=jax.ShapeDtypeStruct((bsz, n), F32),
        grid=(n // tn,),
        in_specs=[pl.BlockSpec((bsz, d), lambda j: (0, 0)),
                  pl.BlockSpec((d, tn), lambda j: (0, j)),
                  pl.BlockSpec((1, tn), lambda j: (0, j))],
        out_specs=pl.BlockSpec((bsz, tn), lambda j: (0, j)),
        compiler_params=_cparams(("arbitrary",)),
        name="ada",
    )(c, w_ada, b_ada.reshape(1, n))


def _proj_kernel(x_ref, sc_ref, sh_ref, w_ref, walr_ref, o_ref, alr_ref, h_scr):
    @pl.when(pl.program_id(1) == 0)
    def _():
        tm = x_ref.shape[0]
        sub = min(LN_ROWS, tm)
        for r0 in range(0, tm, sub):
            rows = slice(r0, r0 + sub)
            h = _ln(x_ref[rows, :]) * (1.0 + sc_ref[...]) + sh_ref[...]
            h_scr[rows, :] = h.astype(BF16)
        alr_ref[...] = jnp.dot(h_scr[...], walr_ref[...], preferred_element_type=F32)

    o_ref[...] = jnp.dot(h_scr[...], w_ref[...],
                         preferred_element_type=F32).astype(BF16)


def _proj(x2, ada3, w_main, w_alr, seq):
    n, d = x2.shape
    cols = w_main.shape[1]
    tm, tn = min(PROJ_TM, seq), PROJ_TN
    per_b = seq // tm
    return pl.pallas_call(
        _proj_kernel,
        out_shape=(jax.ShapeDtypeStruct((n, cols), BF16),
                   jax.ShapeDtypeStruct((n, LANES), F32)),
        grid=(n // tm, cols // tn),
        in_specs=[pl.BlockSpec((tm, d), lambda i, j: (i, 0)),
                  pl.BlockSpec((None, 1, d), lambda i, j: ((i // per_b) * 6 + 1, 0, 0)),
                  pl.BlockSpec((None, 1, d), lambda i, j: ((i // per_b) * 6 + 0, 0, 0)),
                  pl.BlockSpec((d, tn), lambda i, j: (0, j)),
                  pl.BlockSpec((d, LANES), lambda i, j: (0, 0))],
        out_specs=(pl.BlockSpec((tm, tn), lambda i, j: (i, j)),
                   pl.BlockSpec((tm, LANES), lambda i, j: (i, 0))),
        scratch_shapes=[pltpu.VMEM((tm, d), BF16)],
        compiler_params=_cparams(("arbitrary", "arbitrary")),
        name="proj",
    )(x2, ada3, ada3, w_main, w_alr)


def _swa_kernel(sink_ref, q_ref, kp_ref, kc_ref, vp_ref, vc_ref, o_ref):
    n = pl.program_id(1)
    blk = WINDOW
    half = ATT_HEAD_DIM
    group = ATT_HEADS // ATT_KV_HEADS

    lane = lax.broadcasted_iota(I32, (2 * blk, 2 * half), 1)
    lo = lane < half

    def variants(prev_ref, cur_ref, scale):
        band = jnp.concatenate([prev_ref[...], cur_ref[...]], axis=0).astype(F32) * scale
        rolled = pltpu.roll(band, half, axis=1)
        zero = jnp.zeros_like(band)
        return ((jnp.where(lo, band, zero).astype(BF16), jnp.where(lo, zero, rolled).astype(BF16)),
                (jnp.where(lo, rolled, zero).astype(BF16), jnp.where(lo, zero, band).astype(BF16)))

    kvar = variants(kp_ref, kc_ref, ATT_HEAD_DIM ** -0.5)
    vvar = variants(vp_ref, vc_ref, 1.0)

    qi = lax.broadcasted_iota(I32, (blk, 2 * blk), 0)
    si = lax.broadcasted_iota(I32, (blk, 2 * blk), 1)
    dist = qi + blk - si
    first_key = jnp.where(n > 0, 0, blk)
    valid = (dist >= 0) & (dist < WINDOW) & (si >= first_key)
    dist_f = dist.astype(F32)

    for pair in range(ATT_HEADS // 2):
        q_pair = q_ref[:, pair * 2 * half:(pair + 1) * 2 * half]
        acc = jnp.zeros((blk, 2 * half), F32)
        for sub in range(2):
            h = pair * 2 + sub
            kh = h // group
            slope = 2.0 ** (-8.0 * (h + 1.0) / ATT_HEADS)
            s = lax.dot_general(q_pair, kvar[kh][sub], NT_DIMS, preferred_element_type=F32)
            s = jnp.where(valid, s - slope * dist_f, -jnp.inf)
            sink = sink_ref[h]
            m = jnp.maximum(jnp.max(s, axis=-1, keepdims=True), sink)
            p = jnp.exp(s - m)
            denom = jnp.sum(p, axis=-1, keepdims=True) + jnp.exp(sink - m)
            p = p / denom
            acc = acc + jnp.dot(p.astype(BF16), vvar[kh][sub], preferred_element_type=F32)
        o_ref[:, pair * 2 * half:(pair + 1) * 2 * half] = acc.astype(BF16)


def _swa(proj, sinks, bsz, seq, q_blk, k_blk, v_blk):
    n = bsz * seq
    nb = seq // WINDOW
    qw = ATT_HEADS * ATT_HEAD_DIM
    kw = ATT_KV_HEADS * ATT_HEAD_DIM
    cur = lambda col: (lambda b, i: (b * nb + i, col))
    prev = lambda col: (lambda b, i: (b * nb + jnp.maximum(i - 1, 0), col))
    return pl.pallas_call(
        _swa_kernel,
        out_shape=jax.ShapeDtypeStruct((n, qw), BF16),
        grid=(bsz, nb),
        in_specs=[pl.BlockSpec(memory_space=pltpu.SMEM),
                  pl.BlockSpec((WINDOW, qw), cur(q_blk)),
                  pl.BlockSpec((WINDOW, kw), prev(k_blk)),
                  pl.BlockSpec((WINDOW, kw), cur(k_blk)),
                  pl.BlockSpec((WINDOW, kw), prev(v_blk)),
                  pl.BlockSpec((WINDOW, kw), cur(v_blk))],
        out_specs=pl.BlockSpec((WINDOW, qw), lambda b, i: (b * nb + i, 0)),
        compiler_params=_cparams(("arbitrary", "arbitrary")),
        name="swa",
    )(sinks, proj, proj, proj, proj, proj)


def _gla_kernel(q_ref, k_ref, v_ref, r_ref, alr_ref, wg_ref, bg_ref, gain_ref, o_ref, st_scr):
    c = GLA_CHUNK
    hp = st_scr.shape[0]
    ts = q_ref.shape[0]
    dk = q_ref.shape[1] // hp
    dv = v_ref.shape[1] // hp
    nc = ts // c

    @pl.when(pl.program_id(2) == 0)
    def _():
        st_scr[...] = jnp.zeros_like(st_scr)

    z = jnp.dot(alr_ref[...], wg_ref[...], preferred_element_type=F32,
                precision=lax.Precision.HIGHEST) + bg_ref[...]
    log_a = (jnp.minimum(z, 0.0) - jnp.log1p(jnp.exp(-jnp.abs(z)))) / GLA_GATE_TEMP
    row = lax.broadcasted_iota(I32, (ts, ts), 0)
    col = lax.broadcasted_iota(I32, (ts, ts), 1)
    causal = (row >= col) & ((row // c) == (col // c))
    b = jnp.dot(causal.astype(F32), log_a, preferred_element_type=F32,
                precision=lax.Precision.HIGHEST)
    b_last = [b[(ci + 1) * c - 1:(ci + 1) * c, :] for ci in range(nc)]
    b_end = jnp.concatenate([jnp.broadcast_to(bl, (c, hp * dk)) for bl in b_last], axis=0)
    q = q_ref[...].astype(F32)
    k = k_ref[...].astype(F32)
    q_e = (q * jnp.exp(b) * (dk ** -0.5)).astype(BF16)
    k_e = (k * jnp.exp(-b)).astype(BF16)
    k_end = (k * jnp.exp(b_end - b)).astype(BF16)
    decay = [jnp.exp(bl) for bl in b_last]

    outs = []
    for h in range(hp):
        kc = slice(h * dk, (h + 1) * dk)
        v = v_ref[:, h * dv:(h + 1) * dv]
        a = lax.dot_general(q_e[:, kc], k_e[:, kc], NT_DIMS, preferred_element_type=F32)
        a = jnp.where(causal, a, 0.0).astype(BF16)
        o_intra = jnp.dot(a, v, preferred_element_type=F32)
        st = st_scr[h]
        o_inter = []
        for ci in range(nc):
            rows = slice(ci * c, (ci + 1) * c)
            o_inter.append(lax.dot_general(q_e[rows, kc], st.astype(BF16), NT_DIMS,
                                           preferred_element_type=F32))
            upd = lax.dot_general(v[rows, :], k_end[rows, kc], TN_DIMS,
                                  preferred_element_type=F32)
            st = st * decay[ci][:, kc] + upd
        st_scr[h] = st
        o = o_intra + jnp.concatenate(o_inter, axis=0)
        outs.append(o * lax.rsqrt(jnp.mean(o * o, axis=-1, keepdims=True) + LN_EPS))

    o = jnp.concatenate(outs, axis=1) * gain_ref[...]
    r = r_ref[...].astype(F32)
    o_ref[...] = (o * (r * _sigmoid(r))).astype(BF16)


def _gla(proj, alr, wg_pad, bg, gain, bsz, seq, q_blk, k_blk, v_blk, r_blk):
    n = bsz * seq
    hp = GLA_HEADS_PER_STEP
    dk = wg_pad.shape[1] // GLA_HEADS
    dv = gain.shape[1] // GLA_HEADS
    ts = min(GLA_STEP, seq)
    steps = seq // ts
    assert q_blk % hp == 0 and k_blk % hp == 0 and v_blk % hp == 0 and r_blk % hp == 0
    rowmap = lambda col0: (lambda b, h, t: (b * steps + t, col0 // hp + h))
    return pl.pallas_call(
        _gla_kernel,
        out_shape=jax.ShapeDtypeStruct((n, GLA_HEADS * dv), BF16),
        grid=(bsz, GLA_HEADS // hp, steps),
        in_specs=[pl.BlockSpec((ts, hp * dk), rowmap(q_blk)),
                  pl.BlockSpec((ts, hp * dk), rowmap(k_blk)),
                  pl.BlockSpec((ts, hp * dv), rowmap(v_blk)),
                  pl.BlockSpec((ts, hp * dv), rowmap(r_blk)),
                  pl.BlockSpec((ts, LANES), lambda b, h, t: (b * steps + t, 0)),
                  pl.BlockSpec((LANES, hp * dk), lambda b, h, t: (0, h)),
                  pl.BlockSpec((1, hp * dk), lambda b, h, t: (0, h)),
                  pl.BlockSpec((1, hp * dv), lambda b, h, t: (0, h))],
        out_specs=pl.BlockSpec((ts, hp * dv), rowmap(0)),
        scratch_shapes=[pltpu.VMEM((hp, dv, dk), F32)],
        compiler_params=_cparams(("arbitrary", "arbitrary", "arbitrary")),
        name="gla",
    )(proj, proj, proj, proj, alr, wg_pad, bg, gain)


def _merge_kernel(ya_ref, yg_ref, ga_ref, gg_ref, x_ref, gt1_ref, sc2_ref, sh2_ref,
                  pa_ref, pg_ref, wo_ref, g1_ref, b1_ref, wrh_ref, wrl_ref, br_ref,
                  x1_ref, h2_ref, lt_ref):
    a = jnp.dot(ya_ref[...], pa_ref[...], preferred_element_type=F32)
    g = jnp.dot(yg_ref[...], pg_ref[...], preferred_element_type=F32)
    merged = (_sigmoid(ga_ref[...].astype(F32)) * a + _sigmoid(gg_ref[...].astype(F32)) * g)
    mix = jnp.dot(merged.astype(BF16), wo_ref[...], preferred_element_type=F32)
    x1 = _ln(DN_ALPHA * x_ref[...] + gt1_ref[...] * mix) * g1_ref[...] + b1_ref[...]
    x1_ref[...] = x1
    h2 = _ln(x1) * (1.0 + sc2_ref[...]) + sh2_ref[...]
    h2_ref[...] = h2
    h2_hi = h2.astype(BF16)
    h2_lo = (h2 - h2_hi.astype(F32)).astype(BF16)
    wrh = wrh_ref[...]
    logits = (lax.dot_general(wrh, h2_hi, NT_DIMS, preferred_element_type=F32)
              + lax.dot_general(wrh, h2_lo, NT_DIMS, preferred_element_type=F32)
              + lax.dot_general(wrl_ref[...], h2_hi, NT_DIMS, preferred_element_type=F32))
    lt_ref[...] = logits + br_ref[...]


def _merge(y_att, y_gla, proj, x2, ada3, p_a, p_g, w_o, g1, b1, wr_hi, wr_lo, b_r, seq,
           ga_blk, gg_blk):
    n, d = x2.shape
    tm = min(MERGE_TM, seq)
    per_b = seq // tm
    ne = wr_hi.shape[0]
    const = lambda shape: pl.BlockSpec(shape, lambda i: (0,) * len(shape),
                                       pipeline_mode=pl.Buffered(1))
    adarow = lambda k: pl.BlockSpec((None, 1, d), lambda i: ((i // per_b) * 6 + k, 0, 0))
    return pl.pallas_call(
        _merge_kernel,
        out_shape=(jax.ShapeDtypeStruct((n, d), F32),
                   jax.ShapeDtypeStruct((n, d), F32),
                   jax.ShapeDtypeStruct((ne, n), F32)),
        grid=(n // tm,),
        in_specs=[pl.BlockSpec((tm, y_att.shape[1]), lambda i: (i, 0)),
                  pl.BlockSpec((tm, d), lambda i: (i, 0)),
                  pl.BlockSpec((tm, d), lambda i: (i, ga_blk)),
                  pl.BlockSpec((tm, d), lambda i: (i, gg_blk)),
                  pl.BlockSpec((tm, d), lambda i: (i, 0)),
                  adarow(2), adarow(4), adarow(3),
                  const(p_a.shape), const(p_g.shape), const(w_o.shape),
                  const((1, d)), const((1, d)),
                  const(wr_hi.shape), const(wr_lo.shape), const((ne, 1))],
        out_specs=(pl.BlockSpec((tm, d), lambda i: (i, 0)),
                   pl.BlockSpec((tm, d), lambda i: (i, 0)),
                   pl.BlockSpec((ne, tm), lambda i: (0, i))),
        compiler_params=_cparams(("arbitrary",)),
        name="merge",
    )(y_att, y_gla, proj, proj, x2, ada3, ada3, ada3, p_a, p_g, w_o, g1, b1, wr_hi, wr_lo, b_r)


def _route_kernel(lt_ref, dest_ref, w_ref, blke_ref, bend_ref, vend_ref, oh_scr, rank_scr):
    ne, n = lt_ref.shape
    logits = lt_ref[...]
    eidx = lax.broadcasted_iota(I32, (ne, n), 0).astype(F32)
    vals, idxs = [], []
    for _ in range(TOP_K):
        m = jnp.max(logits, axis=0, keepdims=True)
        idx = jnp.min(jnp.where(logits == m, eidx, float(ne)), axis=0, keepdims=True)
        vals.append(m)
        idxs.append(idx)
        logits = jnp.where(eidx == idx, -jnp.inf, logits)
    exps = [jnp.exp(v - vals[0]) for v in vals]
    total = exps[0] + exps[1] + exps[2] + exps[3]
    for k in range(TOP_K):
        w_ref[k:k + 1, :] = exps[k] / total

    onehot = jnp.zeros((ne, n), F32)
    for k in range(TOP_K):
        onehot = onehot + (eidx == idxs[k]).astype(F32)
    oh_scr[...] = onehot.astype(BF16)

    r = lax.broadcasted_iota(I32, (LANES, 2 * LANES), 0)
    cidx = lax.broadcasted_iota(I32, (LANES, 2 * LANES), 1)
    scan_mat = ((r < cidx) | (cidx >= LANES)).astype(BF16)
    carry = jnp.zeros((ne, LANES), F32)
    for t in range(n // LANES):
        cols = slice(t * LANES, (t + 1) * LANES)
        both = jnp.dot(oh_scr[:, cols], scan_mat, preferred_element_type=F32)
        rank_scr[:, cols] = both[:, :LANES] + carry
        carry = carry + both[:, LANES:]

    counts = carry
    nblk = jnp.floor((counts + (MOE_BLOCK - 1)) * (1.0 / MOE_BLOCK))
    er = lax.broadcasted_iota(I32, (ne, ne), 0)
    ec = lax.broadcasted_iota(I32, (ne, ne), 1)
    strict_lower = (ec < er).astype(BF16)
    blk_start = jnp.dot(strict_lower, nblk.astype(BF16), preferred_element_type=F32)
    blk_end = blk_start + nblk
    row_start = blk_start[:, :1] * float(MOE_BLOCK)

    pos = rank_scr[...] + row_start
    for k in range(TOP_K):
        d = jnp.sum(jnp.where(eidx == idxs[k], pos, 0.0), axis=0, keepdims=True)
        dest_ref[k:k + 1, :] = d.astype(I32)

    nb_pad = blke_ref.shape[1]
    j = lax.broadcasted_iota(I32, (ne, nb_pad), 1).astype(F32)
    blke = jnp.sum((blk_end[:, :1] <= j).astype(F32), axis=0, keepdims=True)
    blke_ref[...] = jnp.minimum(blke, float(ne - 1)).astype(I32)
    bend_ref[...] = blk_end.astype(I32)
    vend_ref[...] = (blk_start * float(MOE_BLOCK) + counts).astype(I32)


def _route(logits_t, n_blocks):
    ne, n = logits_t.shape
    nb_pad = -(-n_blocks // LANES) * LANES
    return pl.pallas_call(
        _route_kernel,
        out_shape=(jax.ShapeDtypeStruct((TOP_K, n), I32),
                   jax.ShapeDtypeStruct((TOP_K, n), F32),
                   jax.ShapeDtypeStruct((1, nb_pad), I32),
                   jax.ShapeDtypeStruct((ne, LANES), I32),
                   jax.ShapeDtypeStruct((ne, LANES), I32)),
        scratch_shapes=[pltpu.VMEM((ne, n), BF16), pltpu.VMEM((ne, n), F32)],
        compiler_params=pltpu.CompilerParams(vmem_limit_bytes=VMEM_LIMIT),
        name="route",
    )(logits_t)


def _invert_kernel(dest_ref, vend_ref, bend_ref, inv_ref):
    cap = inv_ref.shape[0]
    n_rows = dest_ref.shape[0]
    ne = bend_ref.shape[0]
    step = pl.program_id(0)
    scat_per = n_rows // (pl.num_programs(0) - 1)

    @pl.when(step == 0)
    def _():
        def pad(p, carry):
            inv_ref[p] = -1
            return carry
        for e in range(ne):
            lax.fori_loop(vend_ref[e], bend_ref[e] * MOE_BLOCK, pad, 0)
        lax.fori_loop(bend_ref[ne - 1] * MOE_BLOCK, cap, pad, 0)

    @pl.when(step >= 1)
    def _():
        def scatter(g, carry):
            b = (step - 1) * scat_per + g * ROW_UNROLL
            for u in range(ROW_UNROLL):
                inv_ref[dest_ref[b + u]] = b + u
            return carry
        lax.fori_loop(0, scat_per // ROW_UNROLL, scatter, 0)


def _invert(dest_flat, valid_end, blk_end, cap):
    steps = INVERT_STEPS
    assert dest_flat.shape[0] % (steps * ROW_UNROLL) == 0
    smem = pl.BlockSpec(memory_space=pltpu.SMEM)
    return pl.pallas_call(
        _invert_kernel,
        out_shape=jax.ShapeDtypeStruct((cap,), I32),
        grid=(1 + steps,),
        in_specs=[smem, smem, smem],
        out_specs=smem,
        compiler_params=pltpu.CompilerParams(dimension_semantics=("arbitrary",)),
        name="invert",
    )(dest_flat, valid_end, blk_end)


GU_CHUNK = 2 * LANES
EXPERT_RING = 5
ROW_DMA_AFTER = (0, 1)


def _expert_kernel(blke_ref, bend_ref, inv_ref, h_ref, wgu_ref, wd_ref, bg_ref, bu_ref, bd_ref,
                   y4_ref, wg_scr, wu_scr, wd_scr, gu_stage, wd_stage, xbuf, ybuf,
                   sems, gsem, ssem, fence):
    j = pl.program_id(0)
    n_blocks = pl.num_programs(0)
    ne = bend_ref.shape[0]
    n_used = bend_ref[ne - 1]
    e = blke_ref[j]
    e_end = bend_ref[e]
    d, f = wg_scr.shape
    bm = xbuf.shape[1]
    n_tok = h_ref.shape[0]
    spare0 = TOP_K * n_tok
    units = f // LANES
    ring = gu_stage.shape[0]
    tf = min(EXPERT_TF, f)
    first = (j == 0) | (e != blke_ref[jnp.maximum(j - 1, 0)])
    last = j == e_end - 1
    has_next = e_end < n_used
    nxt = blke_ref[jnp.minimum(e_end, n_blocks - 1)]
    slot = j % 2
    nslot = 1 - slot

    def gather_row(blk, dslot, r):
        tok = lax.rem(jnp.maximum(inv_ref[blk * bm + r], 0), n_tok)
        pltpu.make_async_copy(h_ref.at[pl.ds(tok, 1), :], xbuf.at[dslot, pl.ds(r, 1), :],
                              gsem.at[dslot]).start()

    def issue_gather(blk, dslot):
        for r in range(bm):
            gather_row(blk, dslot, r)

    def wait_gather(dslot):
        pltpu.make_async_copy(h_ref.at[pl.ds(0, bm), :], xbuf.at[dslot], gsem.at[dslot]).wait()

    def scatter_row(blk, sslot, all_spare, r):
        flat = inv_ref[blk * bm + r]
        row = jnp.where(all_spare | (flat < 0), spare0 + sslot * bm + r, flat)
        pltpu.make_async_copy(ybuf.at[sslot, pl.ds(r, 1), :], y4_ref.at[pl.ds(row, 1), :],
                              ssem.at[sslot]).start()

    def issue_scatter(blk, sslot, all_spare):
        for r in range(bm):
            scatter_row(blk, sslot, all_spare, r)

    def wait_scatter(sslot):
        pltpu.make_async_copy(ybuf.at[sslot], y4_ref.at[pl.ds(0, bm), :], ssem.at[sslot]).wait()

    def gu_copy(ex, u, wslot):
        return pltpu.make_async_copy(wgu_ref.at[ex, :, pl.ds(u * GU_CHUNK, GU_CHUNK)],
                                     gu_stage.at[wslot], sems.at[0, wslot])

    def wd_copy(ex, u, wslot):
        return pltpu.make_async_copy(wd_ref.at[ex, pl.ds(u * LANES, LANES), :],
                                     wd_stage.at[wslot], sems.at[1, wslot])

    def start_unit(ex, u):
        gu_copy(ex, u, u % ring).start()
        wd_copy(ex, u, u % ring).start()

    def convert_unit(ex, u):
        wslot = u % ring
        gu_copy(ex, u, wslot).wait()
        wd_copy(ex, u, wslot).wait()
        src = lax.broadcasted_iota(I32, (GU_CHUNK, GU_CHUNK), 0)
        dst = lax.broadcasted_iota(I32, (GU_CHUNK, GU_CHUNK), 1)
        perm = (dst == (src >> 1) + (src & 1) * LANES).astype(BF16)
        cols = slice(u * LANES, (u + 1) * LANES)
        sep = jnp.dot(gu_stage[wslot].astype(BF16), perm, preferred_element_type=F32)
        wg_scr[:, cols] = sep[:, :LANES].astype(BF16)
        wu_scr[:, cols] = sep[:, LANES:].astype(BF16)
        wd_scr[cols, :] = wd_stage[wslot].astype(BF16)
        if u + ring < units:
            start_unit(ex, u + ring)

    @pl.when(j == 0)
    def _():
        ybuf[...] = jnp.zeros_like(ybuf)
        pltpu.make_async_copy(ybuf.at[0], y4_ref.at[pl.ds(spare0, bm), :], ssem.at[0]).start()
        issue_gather(0, 0)
        for u in range(ring):
            start_unit(e, u)
        for u in range(units):
            convert_unit(e, u)

    @pl.when((j < n_used) & first & has_next)
    def _():
        for u in range(ring):
            start_unit(nxt, u)

    def compute(replace):
        wait_gather(slot)
        x = xbuf[slot].astype(BF16)
        prev_blk = jnp.maximum(j - 1, 0)
        next_blk = jnp.minimum(j + 1, n_blocks - 1)
        jobs = ([lambda r=r: gather_row(next_blk, nslot, r) for r in range(bm)]
                + [lambda r=r: scatter_row(prev_blk, nslot, j == 0, r) for r in range(bm)])
        n_chunks = f // tf
        after = ROW_DMA_AFTER if n_chunks > max(ROW_DMA_AFTER) else (n_chunks - 1,)
        per_point = -(-len(jobs) // len(after))

        def row_jobs():
            for _ in range(min(per_point, len(jobs))):
                jobs.pop(0)()
            pl.semaphore_signal(fence, 1)
            pl.semaphore_wait(fence, 1)

        acc = jnp.zeros((bm, d), F32)
        for c in range(f // tf):
            cols = slice(c * tf, (c + 1) * tf)
            g = jnp.dot(x, wg_scr[:, cols], preferred_element_type=F32) + bg_ref[:, cols]
            u = jnp.dot(x, wu_scr[:, cols], preferred_element_type=F32) + bu_ref[:, cols]
            gate = jnp.minimum(g, SWIGLU_LIMIT)
            up = jnp.clip(u, -SWIGLU_LIMIT, SWIGLU_LIMIT)
            act = (up + 1.0) * gate * _sigmoid(SWIGLU_ALPHA * gate)
            acc = acc + jnp.dot(act.astype(BF16), wd_scr[cols, :], preferred_element_type=F32)
            if c in after:
                row_jobs()
            if replace:
                for unit in range(c * tf // LANES, (c + 1) * tf // LANES):
                    convert_unit(nxt, unit)
        wait_scatter(slot)
        ybuf[slot] = acc + bd_ref[...]

    @pl.when((j < n_used) & last & has_next)
    def _():
        compute(True)

    @pl.when((j < n_used) & jnp.logical_not(last & has_next))
    def _():
        compute(False)

    @pl.when(j == n_used - 1)
    def _():
        issue_scatter(j, slot, False)
        wait_scatter(slot)
        wait_scatter(nslot)
        wait_gather(nslot)


def _experts(blk_e, blk_end, inv, h2, w_gate_up, w_down, b_gate, b_up, b_down):
    n, d = h2.shape
    ne, f, _ = w_down.shape
    n_blocks = blk_e.shape[0]
    ring = min(EXPERT_RING, f // LANES)
    bspec = lambda width: pl.BlockSpec((None, 1, width), lambda j, be, bend, iv: (be[j], 0, 0))
    hbm = pl.BlockSpec(memory_space=pl.ANY)
    return pl.pallas_call(
        _expert_kernel,
        out_shape=jax.ShapeDtypeStruct((TOP_K * n + 2 * MOE_BLOCK, d), F32),
        grid_spec=pltpu.PrefetchScalarGridSpec(
            num_scalar_prefetch=3,
            grid=(n_blocks,),
            in_specs=[hbm, hbm, hbm, bspec(f), bspec(f), bspec(d)],
            out_specs=hbm,
            scratch_shapes=[pltpu.VMEM((d, f), BF16), pltpu.VMEM((d, f), BF16),
                            pltpu.VMEM((f, d), BF16),
                            pltpu.VMEM((ring, d, GU_CHUNK), F32), pltpu.VMEM((ring, LANES, d), F32),
                            pltpu.VMEM((2, MOE_BLOCK, d), F32), pltpu.VMEM((2, MOE_BLOCK, d), F32),
                            pltpu.SemaphoreType.DMA((2, ring)),
                            pltpu.SemaphoreType.DMA((2,)), pltpu.SemaphoreType.DMA((2,)),
                            pltpu.SemaphoreType.REGULAR]),
        compiler_params=_cparams(("arbitrary",)),
        name="experts",
    )(blk_e, blk_end, inv, h2, w_gate_up, w_down, b_gate, b_up, b_down)


def _combine_kernel(y0_ref, y1_ref, y2_ref, y3_ref, wt_ref, x1_ref, gt2_ref, g2_ref, b2_ref, o_ref):
    wt = wt_ref[...]
    ffn = y0_ref[...] * wt[:, 0:1]
    for k, y_ref in enumerate((y1_ref, y2_ref, y3_ref), start=1):
        ffn = ffn + y_ref[...] * wt[:, k:k + 1]
    z = DN_ALPHA * x1_ref[...] + gt2_ref[...] * ffn
    o_ref[...] = _ln(z) * g2_ref[...] + b2_ref[...]


def _combine(y4, w_t, x1, ada3, g2, b2, seq):
    n, d = x1.shape
    tm = min(ROW_TM, seq)
    per_b = seq // tm
    tiles = n // tm
    yspec = lambda k: pl.BlockSpec((tm, d), lambda i: (k * tiles + i, 0))
    return pl.pallas_call(
        _combine_kernel,
        out_shape=jax.ShapeDtypeStruct((n, d), F32),
        grid=(tiles,),
        in_specs=[yspec(0), yspec(1), yspec(2), yspec(3),
                  pl.BlockSpec((tm, TOP_K), lambda i: (i, 0)),
                  pl.BlockSpec((tm, d), lambda i: (i, 0)),
                  pl.BlockSpec((None, 1, d), lambda i: ((i // per_b) * 6 + 5, 0, 0)),
                  pl.BlockSpec((1, d), lambda i: (0, 0)),
                  pl.BlockSpec((1, d), lambda i: (0, 0))],
        out_specs=pl.BlockSpec((tm, d), lambda i: (i, 0)),
        compiler_params=_cparams(("arbitrary",)),
        name="combine",
    )(y4, y4, y4, y4, w_t, x1, ada3, g2, b2)


def kernel(x, c, w_ada, b_ada, w_in, w_gla_gate_up, b_gla_gate, attn_sinks, gla_norm_gain,
           w_branch_att, w_branch_gla, w_out, ln1_gain, ln1_bias, w_router, b_router,
           w_gate_up, b_gate_up, w_down, b_down, ln2_gain, ln2_bias):
    bsz, seq, d = x.shape
    n = bsz * seq
    qa_w = ATT_HEADS * ATT_HEAD_DIM
    kv_w = ATT_KV_HEADS * ATT_HEAD_DIM
    gk_w = d // 2
    splits = (qa_w, kv_w, kv_w, gk_w, gk_w, d, d, GLA_GATE_RANK, d, d)
    offs = [0]
    for wdt in splits:
        offs.append(offs[-1] + wdt)
    ne = w_router.shape[-1]
    f = w_down.shape[-2]

    x2 = x.reshape(n, d)
    for l in range(w_in.shape[0]):
        wl = w_in[l]
        part = lambda i: wl[:, offs[i]:offs[i + 1]]
        w_main = jnp.concatenate([part(3), part(4), part(5), part(6), part(8), part(9),
                                  part(0), part(1), part(2)], axis=1).astype(BF16)
        w_alr = jnp.pad(part(7), ((0, 0), (0, LANES - GLA_GATE_RANK))).astype(BF16)
        wg_pad = jnp.pad(w_gla_gate_up[l], ((0, LANES - GLA_GATE_RANK), (0, 0)))
        wr_t = w_router[l].T
        wr_hi = wr_t.astype(BF16)
        wr_lo = (wr_t - wr_hi.astype(F32)).astype(BF16)
        bgu = b_gate_up[l].reshape(ne, 1, f, 2)

        ada = _ada(c, w_ada[l], b_ada[l])
        ada3 = ada.reshape(bsz * 6, 1, d)

        proj, alr = _proj(x2, ada3, w_main, w_alr, seq)
        y_att = _swa(proj, attn_sinks[l], bsz, seq,
                     q_blk=(2 * gk_w + 4 * d) // qa_w,
                     k_blk=(2 * gk_w + 4 * d + qa_w) // kv_w,
                     v_blk=(2 * gk_w + 4 * d + qa_w + kv_w) // kv_w)
        dk = gk_w // GLA_HEADS
        dv = d // GLA_HEADS
        y_gla = _gla(proj, alr, wg_pad, b_gla_gate[l].reshape(1, gk_w),
                     gla_norm_gain[l].reshape(1, d), bsz, seq,
                     q_blk=0, k_blk=gk_w // dk, v_blk=2 * gk_w // dv, r_blk=(2 * gk_w + d) // dv)
        x1, h2, logits_t = _merge(
            y_att, y_gla, proj, x2, ada3,
            w_branch_att[l].astype(BF16), w_branch_gla[l].astype(BF16), w_out[l].astype(BF16),
            ln1_gain[l].reshape(1, d), ln1_bias[l].reshape(1, d),
            wr_hi, wr_lo, b_router[l].reshape(ne, 1), seq,
            ga_blk=(2 * gk_w + 2 * d) // d, gg_blk=(2 * gk_w + 3 * d) // d)

        n_rows = n * TOP_K
        cap = -(-n_rows // MOE_BLOCK) * MOE_BLOCK + ne * MOE_BLOCK
        n_blocks = cap // MOE_BLOCK
        dest, w_top, blk_e, blk_end, valid_end = _route(logits_t, n_blocks)
        dest_flat = dest.reshape(-1)
        blk_e = blk_e.reshape(-1)[:n_blocks]
        blk_end = blk_end[:, 0]
        inv = _invert(dest_flat, valid_end[:, 0], blk_end, cap)
        y4 = _experts(blk_e, blk_end, inv, h2, w_gate_up[l], w_down[l],
                      bgu[..., 0], bgu[..., 1], b_down[l].reshape(ne, 1, d))
        x2 = _combine(y4, w_top.T, x1, ada3,
                      ln2_gain[l].reshape(1, d), ln2_bias[l].reshape(1, d), seq)
    return x2.reshape(bsz, seq, d)
```

```python
import jax
import jax.numpy as jnp
from jax import lax
from jax.experimental import pallas as pl
from jax.experimental.pallas import tpu as pltpu

F32 = jnp.float32
BF16 = jnp.bfloat16
I32 = jnp.int32

ATT_HEADS = 16
ATT_KV_HEADS = 2
ATT_HEAD_DIM = 64
WINDOW = 128
GLA_HEADS = 4
GLA_GATE_RANK = 16
GLA_GATE_TEMP = 16.0
GLA_CHUNK = 64
N_EXPERTS = 32
TOP_K = 4
SWIGLU_LIMIT = 7.0
SWIGLU_ALPHA = 1.702
LN_EPS = 1e-5
DEPTH = 1
DN_ALPHA = (2 * DEPTH) ** 0.25

LANES = 128
VMEM_LIMIT = 56 * 1024 * 1024

MOE_BLOCK = 256
PROJ_TM = 1024
PROJ_TN = 1152
GLA_STEP = 256
GLA_HEADS_PER_STEP = 4
MERGE_TM = 256
MERGE_SUBTILES = 2
ROW_TM = 256
ROW_UNROLL = 8
INVERT_STEPS = 32
EXPERT_TF = 1024
LN_ROWS = 256

NT_DIMS = (((1,), (1,)), ((), ()))
TN_DIMS = (((0,), (0,)), ((), ()))


def _cparams(sem, vmem=VMEM_LIMIT):
    return pltpu.CompilerParams(dimension_semantics=sem, vmem_limit_bytes=vmem)


def _ln(x):
    mu = jnp.mean(x, axis=-1, keepdims=True)
    xc = x - mu
    var = jnp.mean(xc * xc, axis=-1, keepdims=True)
    return xc * lax.rsqrt(var + LN_EPS)


def _sigmoid(x):
    return 1.0 / (1.0 + jnp.exp(-x))


def _ada_kernel(c_ref, w_ref, b_ref, o_ref):
    c = c_ref[...]
    s = c * _sigmoid(c)
    o_ref[...] = jnp.dot(s.astype(BF16), w_ref[...].astype(BF16),
                         preferred_element_type=F32) + b_ref[...]


def _ada(c, w_ada, b_ada):
    bsz, d = c.shape
    n = w_ada.shape[1]
    tn = 1536
    return pl.pallas_call(
        _ada_kernel,
        out_shape=jax.ShapeDtypeStruct((bsz, n), F32),
        grid=(n // tn,),
        in_specs=[pl.BlockSpec((bsz, d), lambda j: (0, 0)),
                  pl.BlockSpec((d, tn), lambda j: (0, j)),
                  pl.BlockSpec((1, tn), lambda j: (0, j))],
        out_specs=pl.BlockSpec((bsz, tn), lambda j: (0, j)),
        compiler_params=_cparams(("arbitrary",)),
        name="ada",
    )(c, w_ada, b_ada.reshape(1, n))


def _proj_kernel(x_ref, sc_ref, sh_ref, w_ref, walr_ref, o_ref, alr_ref, h_scr):
    @pl.when(pl.program_id(1) == 0)
    def _():
        tm = x_ref.shape[0]
        sub = min(LN_ROWS, tm)
        for r0 in range(0, tm, sub):
            rows = slice(r0, r0 + sub)
            h = _ln(x_ref[rows, :]) * (1.0 + sc_ref[...]) + sh_ref[...]
            h_scr[rows, :] = h.astype(BF16)
        alr_ref[...] = jnp.dot(h_scr[...], walr_ref[...], preferred_element_type=F32)

    o_ref[...] = jnp.dot(h_scr[...], w_ref[...],
                         preferred_element_type=F32).astype(BF16)


def _proj(x2, ada3, w_main, w_alr, seq):
    n, d = x2.shape
    cols = w_main.shape[1]
    tm, tn = min(PROJ_TM, seq), PROJ_TN
    per_b = seq // tm
    return pl.pallas_call(
        _proj_kernel,
        out_shape=(jax.ShapeDtypeStruct((n, cols), BF16),
                   jax.ShapeDtypeStruct((n, LANES), F32)),
        grid=(n // tm, cols // tn),
        in_specs=[pl.BlockSpec((tm, d), lambda i, j: (i, 0)),
                  pl.BlockSpec((None, 1, d), lambda i, j: ((i // per_b) * 6 + 1, 0, 0)),
                  pl.BlockSpec((None, 1, d), lambda i, j: ((i // per_b) * 6 + 0, 0, 0)),
                  pl.BlockSpec((d, tn), lambda i, j: (0, j)),
                  pl.BlockSpec((d, LANES), lambda i, j: (0, 0))],
        out_specs=(pl.BlockSpec((tm, tn), lambda i, j: (i, j)),
                   pl.BlockSpec((tm, LANES), lambda i, j: (i, 0))),
        scratch_shapes=[pltpu.VMEM((tm, d), BF16)],
        compiler_params=_cparams(("arbitrary", "arbitrary")),
        name="proj",
    )(x2, ada3, ada3, w_main, w_alr)


def _swa_kernel(sink_ref, q_ref, kp_ref, kc_ref, vp_ref, vc_ref, o_ref):
    n = pl.program_id(1)
    blk = WINDOW
    half = ATT_HEAD_DIM
    group = ATT_HEADS // ATT_KV_HEADS

    lane = lax.broadcasted_iota(I32, (2 * blk, 2 * half), 1)
    lo = lane < half

    def variants(prev_ref, cur_ref, scale):
        band = jnp.concatenate([prev_ref[...], cur_ref[...]], axis=0).astype(F32) * scale
        rolled = pltpu.roll(band, half, axis=1)
        zero = jnp.zeros_like(band)
        return ((jnp.where(lo, band, zero).astype(BF16), jnp.where(lo, zero, rolled).astype(BF16)),
                (jnp.where(lo, rolled, zero).astype(BF16), jnp.where(lo, zero, band).astype(BF16)))

    kvar = variants(kp_ref, kc_ref, ATT_HEAD_DIM ** -0.5)
    vvar = variants(vp_ref, vc_ref, 1.0)

    qi = lax.broadcasted_iota(I32, (blk, 2 * blk), 0)
    si = lax.broadcasted_iota(I32, (blk, 2 * blk), 1)
    dist = qi + blk - si
    first_key = jnp.where(n > 0, 0, blk)
    valid = (dist >= 0) & (dist < WINDOW) & (si >= first_key)
    dist_f = dist.astype(F32)

    for pair in range(ATT_HEADS // 2):
        q_pair = q_ref[:, pair * 2 * half:(pair + 1) * 2 * half]
        acc = jnp.zeros((blk, 2 * half), F32)
        for sub in range(2):
            h = pair * 2 + sub
            kh = h // group
            slope = 2.0 ** (-8.0 * (h + 1.0) / ATT_HEADS)
            s = lax.dot_general(q_pair, kvar[kh][sub], NT_DIMS, preferred_element_type=F32)
            s = jnp.where(valid, s - slope * dist_f, -jnp.inf)
            sink = sink_ref[h]
            m = jnp.maximum(jnp.max(s, axis=-1, keepdims=True), sink)
            p = jnp.exp(s - m)
            denom = jnp.sum(p, axis=-1, keepdims=True) + jnp.exp(sink - m)
            p = p / denom
            acc = acc + jnp.dot(p.astype(BF16), vvar[kh][sub], preferred_element_type=F32)
        o_ref[:, pair * 2 * half:(pair + 1) * 2 * half] = acc.astype(BF16)


def _swa(proj, sinks, bsz, seq, q_blk, k_blk, v_blk):
    n = bsz * seq
    nb = seq // WINDOW
    qw = ATT_HEADS * ATT_HEAD_DIM
    kw = ATT_KV_HEADS * ATT_HEAD_DIM
    cur = lambda col: (lambda b, i: (b * nb + i, col))
    prev = lambda col: (lambda b, i: (b * nb + jnp.maximum(i - 1, 0), col))
    return pl.pallas_call(
        _swa_kernel,
        out_shape=jax.ShapeDtypeStruct((n, qw), BF16),
        grid=(bsz, nb),
        in_specs=[pl.BlockSpec(memory_space=pltpu.SMEM),
                  pl.BlockSpec((WINDOW, qw), cur(q_blk)),
                  pl.BlockSpec((WINDOW, kw), prev(k_blk)),
                  pl.BlockSpec((WINDOW, kw), cur(k_blk)),
                  pl.BlockSpec((WINDOW, kw), prev(v_blk)),
                  pl.BlockSpec((WINDOW, kw), cur(v_blk))],
        out_specs=pl.BlockSpec((WINDOW, qw), lambda b, i: (b * nb + i, 0)),
        compiler_params=_cparams(("arbitrary", "arbitrary")),
        name="swa",
    )(sinks, proj, proj, proj, proj, proj)


def _gla_kernel(q_ref, k_ref, v_ref, r_ref, alr_ref, wg_ref, bg_ref, gain_ref, o_ref, st_scr):
    c = GLA_CHUNK
    hp = st_scr.shape[0]
    ts = q_ref.shape[0]
    dk = q_ref.shape[1] // hp
    dv = v_ref.shape[1] // hp
    nc = ts // c

    @pl.when(pl.program_id(2) == 0)
    def _():
        st_scr[...] = jnp.zeros_like(st_scr)

    z = jnp.dot(alr_ref[...], wg_ref[...], preferred_element_type=F32,
                precision=lax.Precision.HIGHEST) + bg_ref[...]
    log_a = (jnp.minimum(z, 0.0) - jnp.log1p(jnp.exp(-jnp.abs(z)))) / GLA_GATE_TEMP
    row = lax.broadcasted_iota(I32, (ts, ts), 0)
    col = lax.broadcasted_iota(I32, (ts, ts), 1)
    causal = (row >= col) & ((row // c) == (col // c))
    b = jnp.dot(causal.astype(F32), log_a, preferred_element_type=F32,
                precision=lax.Precision.HIGHEST)
    b_last = [b[(ci + 1) * c - 1:(ci + 1) * c, :] for ci in range(nc)]
    b_end = jnp.concatenate([jnp.broadcast_to(bl, (c, hp * dk)) for bl in b_last], axis=0)
    q = q_ref[...].astype(F32)
    k = k_ref[...].astype(F32)
    q_e = (q * jnp.exp(b) * (dk ** -0.5)).astype(BF16)
    k_e = (k * jnp.exp(-b)).astype(BF16)
    k_end = (k * jnp.exp(b_end - b)).astype(BF16)
    decay = [jnp.exp(bl) for bl in b_last]

    outs = []
    for h in range(hp):
        kc = slice(h * dk, (h + 1) * dk)
        v = v_ref[:, h * dv:(h + 1) * dv]
        a = lax.dot_general(q_e[:, kc], k_e[:, kc], NT_DIMS, preferred_element_type=F32)
        a = jnp.where(causal, a, 0.0).astype(BF16)
        o_intra = jnp.dot(a, v, preferred_element_type=F32)
        st = st_scr[h]
        o_inter = []
        for ci in range(nc):
            rows = slice(ci * c, (ci + 1) * c)
            o_inter.append(lax.dot_general(q_e[rows, kc], st.astype(BF16), NT_DIMS,
                                           preferred_element_type=F32))
            upd = lax.dot_general(v[rows, :], k_end[rows, kc], TN_DIMS,
                                  preferred_element_type=F32)
            st = st * decay[ci][:, kc] + upd
        st_scr[h] = st
        o = o_intra + jnp.concatenate(o_inter, axis=0)
        outs.append(o * lax.rsqrt(jnp.mean(o * o, axis=-1, keepdims=True) + LN_EPS))

    o = jnp.concatenate(outs, axis=1) * gain_ref[...]
    r = r_ref[...].astype(F32)
    o_ref[...] = (o * (r * _sigmoid(r))).astype(BF16)


def _gla(proj, alr, wg_pad, bg, gain, bsz, seq, q_blk, k_blk, v_blk, r_blk):
    n = bsz * seq
    hp = GLA_HEADS_PER_STEP
    dk = wg_pad.shape[1] // GLA_HEADS
    dv = gain.shape[1] // GLA_HEADS
    ts = min(GLA_STEP, seq)
    steps = seq // ts
    assert q_blk % hp == 0 and k_blk % hp == 0 and v_blk % hp == 0 and r_blk % hp == 0
    rowmap = lambda col0: (lambda b, h, t: (b * steps + t, col0 // hp + h))
    return pl.pallas_call(
        _gla_kernel,
        out_shape=jax.ShapeDtypeStruct((n, GLA_HEADS * dv), BF16),
        grid=(bsz, GLA_HEADS // hp, steps),
        in_specs=[pl.BlockSpec((ts, hp * dk), rowmap(q_blk)),
                  pl.BlockSpec((ts, hp * dk), rowmap(k_blk)),
                  pl.BlockSpec((ts, hp * dv), rowmap(v_blk)),
                  pl.BlockSpec((ts, hp * dv), rowmap(r_blk)),
                  pl.BlockSpec((ts, LANES), lambda b, h, t: (b * steps + t, 0)),
                  pl.BlockSpec((LANES, hp * dk), lambda b, h, t: (0, h)),
                  pl.BlockSpec((1, hp * dk), lambda b, h, t: (0, h)),
                  pl.BlockSpec((1, hp * dv), lambda b, h, t: (0, h))],
        out_specs=pl.BlockSpec((ts, hp * dv), rowmap(0)),
        scratch_shapes=[pltpu.VMEM((hp, dv, dk), F32)],
        compiler_params=_cparams(("arbitrary", "arbitrary", "arbitrary")),
        name="gla",
    )(proj, proj, proj, proj, alr, wg_pad, bg, gain)


def _merge_kernel(ya_ref, yg_ref, ga_ref, gg_ref, x_ref, gt1_ref, sc2_ref, sh2_ref,
                  pa_ref, pg_ref, wo_ref, g1_ref, b1_ref, wrh_ref, wrl_ref, br_ref,
                  x1_ref, h2_ref, lt_ref):
    tm = x_ref.shape[0]
    sub = tm // MERGE_SUBTILES
    for s in range(MERGE_SUBTILES):
        rows = slice(s * sub, (s + 1) * sub)
        a = jnp.dot(ya_ref[rows, :], pa_ref[...], preferred_element_type=F32)
        g = jnp.dot(yg_ref[rows, :], pg_ref[...], preferred_element_type=F32)
        merged = (_sigmoid(ga_ref[rows, :].astype(F32)) * a
                  + _sigmoid(gg_ref[rows, :].astype(F32)) * g)
        mix = jnp.dot(merged.astype(BF16), wo_ref[...], preferred_element_type=F32)
        x1 = _ln(DN_ALPHA * x_ref[rows, :] + gt1_ref[...] * mix) * g1_ref[...] + b1_ref[...]
        x1_ref[rows, :] = x1
        h2 = _ln(x1) * (1.0 + sc2_ref[...]) + sh2_ref[...]
        h2_ref[rows, :] = h2
        h2_hi = h2.astype(BF16)
        h2_lo = (h2 - h2_hi.astype(F32)).astype(BF16)
        wrh = wrh_ref[...]
        logits = (lax.dot_general(wrh, h2_hi, NT_DIMS, preferred_element_type=F32)
                  + lax.dot_general(wrh, h2_lo, NT_DIMS, preferred_element_type=F32)
                  + lax.dot_general(wrl_ref[...], h2_hi, NT_DIMS, preferred_element_type=F32))
        lt_ref[:, rows] = logits + br_ref[...]


def _merge(y_att, y_gla, proj, x2, ada3, p_a, p_g, w_o, g1, b1, wr_hi, wr_lo, b_r, seq,
           ga_blk, gg_blk):
    n, d = x2.shape
    tm = min(MERGE_TM, seq)
    per_b = seq // tm
    ne = wr_hi.shape[0]
    const = lambda shape: pl.BlockSpec(shape, lambda i: (0,) * len(shape),
                                       pipeline_mode=pl.Buffered(1))
    adarow = lambda k: pl.BlockSpec((None, 1, d), lambda i: ((i // per_b) * 6 + k, 0, 0))
    return pl.pallas_call(
        _merge_kernel,
        out_shape=(jax.ShapeDtypeStruct((n, d), F32),
                   jax.ShapeDtypeStruct((n, d), F32),
                   jax.ShapeDtypeStruct((ne, n), F32)),
        grid=(n // tm,),
        in_specs=[pl.BlockSpec((tm, y_att.shape[1]), lambda i: (i, 0)),
                  pl.BlockSpec((tm, d), lambda i: (i, 0)),
                  pl.BlockSpec((tm, d), lambda i: (i, ga_blk)),
                  pl.BlockSpec((tm, d), lambda i: (i, gg_blk)),
                  pl.BlockSpec((tm, d), lambda i: (i, 0)),
                  adarow(2), adarow(4), adarow(3),
                  const(p_a.shape), const(p_g.shape), const(w_o.shape),
                  const((1, d)), const((1, d)),
                  const(wr_hi.shape), const(wr_lo.shape), const((ne, 1))],
        out_specs=(pl.BlockSpec((tm, d), lambda i: (i, 0)),
                   pl.BlockSpec((tm, d), lambda i: (i, 0)),
                   pl.BlockSpec((ne, tm), lambda i: (0, i))),
        compiler_params=_cparams(("arbitrary",)),
        name="merge",
    )(y_att, y_gla, proj, proj, x2, ada3, ada3, ada3, p_a, p_g, w_o, g1, b1, wr_hi, wr_lo, b_r)


def _route_kernel(lt_ref, dest_ref, w_ref, blke_ref, bend_ref, vend_ref, oh_scr, rank_scr):
    ne, n = lt_ref.shape
    logits = lt_ref[...]
    eidx = lax.broadcasted_iota(I32, (ne, n), 0).astype(F32)
    vals, idxs = [], []
    for _ in range(TOP_K):
        m = jnp.max(logits, axis=0, keepdims=True)
        idx = jnp.min(jnp.where(logits == m, eidx, float(ne)), axis=0, keepdims=True)
        vals.append(m)
        idxs.append(idx)
        logits = jnp.where(eidx == idx, -jnp.inf, logits)
    exps = [jnp.exp(v - vals[0]) for v in vals]
    total = exps[0] + exps[1] + exps[2] + exps[3]
    for k in range(TOP_K):
        w_ref[k:k + 1, :] = exps[k] / total

    onehot = jnp.zeros((ne, n), F32)
    for k in range(TOP_K):
        onehot = onehot + (eidx == idxs[k]).astype(F32)
    oh_scr[...] = onehot.astype(BF16)

    r = lax.broadcasted_iota(I32, (LANES, 2 * LANES), 0)
    cidx = lax.broadcasted_iota(I32, (LANES, 2 * LANES), 1)
    scan_mat = ((r < cidx) | (cidx >= LANES)).astype(BF16)
    carry = jnp.zeros((ne, LANES), F32)
    for t in range(n // LANES):
        cols = slice(t * LANES, (t + 1) * LANES)
        both = jnp.dot(oh_scr[:, cols], scan_mat, preferred_element_type=F32)
        rank_scr[:, cols] = both[:, :LANES] + carry
        carry = carry + both[:, LANES:]

    counts = carry
    nblk = jnp.floor((counts + (MOE_BLOCK - 1)) * (1.0 / MOE_BLOCK))
    er = lax.broadcasted_iota(I32, (ne, ne), 0)
    ec = lax.broadcasted_iota(I32, (ne, ne), 1)
    strict_lower = (ec < er).astype(BF16)
    blk_start = jnp.dot(strict_lower, nblk.astype(BF16), preferred_element_type=F32)
    blk_end = blk_start + nblk
    row_start = blk_start[:, :1] * float(MOE_BLOCK)

    pos = rank_scr[...] + row_start
    for k in range(TOP_K):
        d = jnp.sum(jnp.where(eidx == idxs[k], pos, 0.0), axis=0, keepdims=True)
        dest_ref[k:k + 1, :] = d.astype(I32)

    nb_pad = blke_ref.shape[1]
    j = lax.broadcasted_iota(I32, (ne, nb_pad), 1).astype(F32)
    blke = jnp.sum((blk_end[:, :1] <= j).astype(F32), axis=0, keepdims=True)
    blke_ref[...] = jnp.minimum(blke, float(ne - 1)).astype(I32)
    bend_ref[...] = blk_end.astype(I32)
    vend_ref[...] = (blk_start * float(MOE_BLOCK) + counts).astype(I32)


def _route(logits_t, n_blocks):
    ne, n = logits_t.shape
    nb_pad = -(-n_blocks // LANES) * LANES
    return pl.pallas_call(
        _route_kernel,
        out_shape=(jax.ShapeDtypeStruct((TOP_K, n), I32),
                   jax.ShapeDtypeStruct((TOP_K, n), F32),
                   jax.ShapeDtypeStruct((1, nb_pad), I32),
                   jax.ShapeDtypeStruct((ne, LANES), I32),
                   jax.ShapeDtypeStruct((ne, LANES), I32)),
        scratch_shapes=[pltpu.VMEM((ne, n), BF16), pltpu.VMEM((ne, n), F32)],
        compiler_params=pltpu.CompilerParams(vmem_limit_bytes=VMEM_LIMIT),
        name="route",
    )(logits_t)


def _invert_kernel(dest_ref, vend_ref, bend_ref, inv_ref):
    cap = inv_ref.shape[0]
    n_rows = dest_ref.shape[0]
    ne = bend_ref.shape[0]
    step = pl.program_id(0)
    scat_per = n_rows // (pl.num_programs(0) - 1)

    @pl.when(step == 0)
    def _():
        def pad(p, carry):
            inv_ref[p] = -1
            return carry
        for e in range(ne):
            lax.fori_loop(vend_ref[e], bend_ref[e] * MOE_BLOCK, pad, 0)
        lax.fori_loop(bend_ref[ne - 1] * MOE_BLOCK, cap, pad, 0)

    @pl.when(step >= 1)
    def _():
        def scatter(g, carry):
            b = (step - 1) * scat_per + g * ROW_UNROLL
            for u in range(ROW_UNROLL):
                inv_ref[dest_ref[b + u]] = b + u
            return carry
        lax.fori_loop(0, scat_per // ROW_UNROLL, scatter, 0)


def _invert(dest_flat, valid_end, blk_end, cap):
    steps = INVERT_STEPS
    assert dest_flat.shape[0] % (steps * ROW_UNROLL) == 0
    smem = pl.BlockSpec(memory_space=pltpu.SMEM)
    return pl.pallas_call(
        _invert_kernel,
        out_shape=jax.ShapeDtypeStruct((cap,), I32),
        grid=(1 + steps,),
        in_specs=[smem, smem, smem],
        out_specs=smem,
        compiler_params=pltpu.CompilerParams(dimension_semantics=("arbitrary",)),
        name="invert",
    )(dest_flat, valid_end, blk_end)


GU_CHUNK = 2 * LANES
EXPERT_RING = 5


def _expert_kernel(blke_ref, bend_ref, inv_ref, h_ref, wgu_ref, wd_ref, bg_ref, bu_ref, bd_ref,
                   y4_ref, wg_scr, wu_scr, wd_scr, gu_stage, wd_stage, xbuf, ybuf,
                   sems, gsem, ssem, fence):
    j = pl.program_id(0)
    n_blocks = pl.num_programs(0)
    ne = bend_ref.shape[0]
    n_used = bend_ref[ne - 1]
    e = blke_ref[j]
    e_end = bend_ref[e]
    d, f = wg_scr.shape
    bm = xbuf.shape[1]
    n_tok = h_ref.shape[0]
    spare0 = TOP_K * n_tok
    units = f // LANES
    ring = gu_stage.shape[0]
    first = (j == 0) | (e != blke_ref[jnp.maximum(j - 1, 0)])
    last = j == e_end - 1
    has_next = e_end < n_used
    nxt = blke_ref[jnp.minimum(e_end, n_blocks - 1)]
    slot = j % 2
    nslot = 1 - slot

    def gather_row(blk, dslot, r):
        tok = lax.rem(jnp.maximum(inv_ref[blk * bm + r], 0), n_tok)
        pltpu.make_async_copy(h_ref.at[pl.ds(tok, 1), :], xbuf.at[dslot, pl.ds(r, 1), :],
                              gsem.at[dslot]).start()

    def issue_gather(blk, dslot):
        for r in range(bm):
            gather_row(blk, dslot, r)

    def wait_gather(dslot):
        pltpu.make_async_copy(h_ref.at[pl.ds(0, bm), :], xbuf.at[dslot], gsem.at[dslot]).wait()

    def scatter_row(blk, sslot, all_spare, r):
        flat = inv_ref[blk * bm + r]
        row = jnp.where(all_spare | (flat < 0), spare0 + sslot * bm + r, flat)
        pltpu.make_async_copy(ybuf.at[sslot, pl.ds(r, 1), :], y4_ref.at[pl.ds(row, 1), :],
                              ssem.at[sslot]).start()

    def issue_scatter(blk, sslot, all_spare):
        for r in range(bm):
            scatter_row(blk, sslot, all_spare, r)

    def wait_scatter(sslot):
        pltpu.make_async_copy(ybuf.at[sslot], y4_ref.at[pl.ds(0, bm), :], ssem.at[sslot]).wait()

    def gu_copy(ex, u, wslot):
        return pltpu.make_async_copy(wgu_ref.at[ex, :, pl.ds(u * GU_CHUNK, GU_CHUNK)],
                                     gu_stage.at[wslot], sems.at[0, wslot])

    def wd_copy(ex, u, wslot):
        return pltpu.make_async_copy(wd_ref.at[ex, pl.ds(u * LANES, LANES), :],
                                     wd_stage.at[wslot], sems.at[1, wslot])

    def start_unit(ex, u):
        gu_copy(ex, u, u % ring).start()
        wd_copy(ex, u, u % ring).start()

    def convert_unit(ex, u):
        wslot = u % ring
        gu_copy(ex, u, wslot).wait()
        wd_copy(ex, u, wslot).wait()
        src = lax.broadcasted_iota(I32, (GU_CHUNK, GU_CHUNK), 0)
        dst = lax.broadcasted_iota(I32, (GU_CHUNK, GU_CHUNK), 1)
        perm = (dst == (src >> 1) + (src & 1) * LANES).astype(BF16)
        cols = slice(u * LANES, (u + 1) * LANES)
        sep = jnp.dot(gu_stage[wslot].astype(BF16), perm, preferred_element_type=F32)
        wg_scr[:, cols] = sep[:, :LANES].astype(BF16)
        wu_scr[:, cols] = sep[:, LANES:].astype(BF16)
        wd_scr[cols, :] = wd_stage[wslot].astype(BF16)
        if u + ring < units:
            start_unit(ex, u + ring)

    @pl.when(j == 0)
    def _():
        ybuf[...] = jnp.zeros_like(ybuf)
        pltpu.make_async_copy(ybuf.at[0], y4_ref.at[pl.ds(spare0, bm), :], ssem.at[0]).start()
        issue_gather(0, 0)
        for u in range(ring):
            start_unit(e, u)
        for u in range(units):
            convert_unit(e, u)

    @pl.when((j < n_used) & first & has_next)
    def _():
        for u in range(ring):
            start_unit(nxt, u)

    def compute(replace):
        wait_gather(slot)
        x = xbuf[slot].astype(BF16)
        prev_blk = jnp.maximum(j - 1, 0)
        next_blk = jnp.minimum(j + 1, n_blocks - 1)
        jobs = ([lambda r=r: gather_row(next_blk, nslot, r) for r in range(bm)]
                + [lambda r=r: scatter_row(prev_blk, nslot, j == 0, r) for r in range(bm)])
        tf = min(EXPERT_TF, f)
        n_chunks = f // tf
        after = tuple(sorted({n_chunks // 4, n_chunks // 2}))
        per_point = -(-len(jobs) // len(after))

        def row_jobs():
            for _ in range(min(per_point, len(jobs))):
                jobs.pop(0)()
            pl.semaphore_signal(fence, 1)
            pl.semaphore_wait(fence, 1)

        acc = jnp.zeros((bm, d), F32)
        for c in range(n_chunks):
            cols = slice(c * tf, (c + 1) * tf)
            g = jnp.dot(x, wg_scr[:, cols], preferred_element_type=F32) + bg_ref[:, cols]
            u = jnp.dot(x, wu_scr[:, cols], preferred_element_type=F32) + bu_ref[:, cols]
            gate = jnp.minimum(g, SWIGLU_LIMIT)
            up = jnp.clip(u, -SWIGLU_LIMIT, SWIGLU_LIMIT)
            act = (up + 1.0) * gate * _sigmoid(SWIGLU_ALPHA * gate)
            acc = acc + jnp.dot(act.astype(BF16), wd_scr[cols, :], preferred_element_type=F32)
            if c in after:
                row_jobs()
            if replace:
                for unit in range(c * tf // LANES, (c + 1) * tf // LANES):
                    convert_unit(nxt, unit)
        wait_scatter(slot)
        ybuf[slot] = acc + bd_ref[...]

    @pl.when((j < n_used) & last & has_next)
    def _():
        compute(True)

    @pl.when((j < n_used) & jnp.logical_not(last & has_next))
    def _():
        compute(False)

    @pl.when(j == n_used - 1)
    def _():
        issue_scatter(j, slot, False)
        wait_scatter(slot)
        wait_scatter(nslot)
        wait_gather(nslot)


def _experts(blk_e, blk_end, inv, h2, w_gate_up, w_down, b_gate, b_up, b_down):
    n, d = h2.shape
    ne, f, _ = w_down.shape
    n_blocks = blk_e.shape[0]
    ring = min(EXPERT_RING, f // LANES)
    bspec = lambda width: pl.BlockSpec((None, 1, width), lambda j, be, bend, iv: (be[j], 0, 0))
    hbm = pl.BlockSpec(memory_space=pl.ANY)
    return pl.pallas_call(
        _expert_kernel,
        out_shape=jax.ShapeDtypeStruct((TOP_K * n + 2 * MOE_BLOCK, d), F32),
        grid_spec=pltpu.PrefetchScalarGridSpec(
            num_scalar_prefetch=3,
            grid=(n_blocks,),
            in_specs=[hbm, hbm, hbm, bspec(f), bspec(f), bspec(d)],
            out_specs=hbm,
            scratch_shapes=[pltpu.VMEM((d, f), BF16), pltpu.VMEM((d, f), BF16),
                            pltpu.VMEM((f, d), BF16),
                            pltpu.VMEM((ring, d, GU_CHUNK), F32), pltpu.VMEM((ring, LANES, d), F32),
                            pltpu.VMEM((2, MOE_BLOCK, d), F32), pltpu.VMEM((2, MOE_BLOCK, d), F32),
                            pltpu.SemaphoreType.DMA((2, ring)),
                            pltpu.SemaphoreType.DMA((2,)), pltpu.SemaphoreType.DMA((2,)),
                            pltpu.SemaphoreType.REGULAR]),
        compiler_params=_cparams(("arbitrary",)),
        name="experts",
    )(blk_e, blk_end, inv, h2, w_gate_up, w_down, b_gate, b_up, b_down)


def _combine_kernel(y0_ref, y1_ref, y2_ref, y3_ref, wt_ref, x1_ref, gt2_ref, g2_ref, b2_ref, o_ref):
    wt = wt_ref[...]
    ffn = y0_ref[...] * wt[:, 0:1]
    for k, y_ref in enumerate((y1_ref, y2_ref, y3_ref), start=1):
        ffn = ffn + y_ref[...] * wt[:, k:k + 1]
    z = DN_ALPHA * x1_ref[...] + gt2_ref[...] * ffn
    o_ref[...] = _ln(z) * g2_ref[...] + b2_ref[...]


def _combine(y4, w_t, x1, ada3, g2, b2, seq):
    n, d = x1.shape
    tm = min(ROW_TM, seq)
    per_b = seq // tm
    tiles = n // tm
    yspec = lambda k: pl.BlockSpec((tm, d), lambda i: (k * tiles + i, 0))
    return pl.pallas_call(
        _combine_kernel,
        out_shape=jax.ShapeDtypeStruct((n, d), F32),
        grid=(tiles,),
        in_specs=[yspec(0), yspec(1), yspec(2), yspec(3),
                  pl.BlockSpec((tm, TOP_K), lambda i: (i, 0)),
                  pl.BlockSpec((tm, d), lambda i: (i, 0)),
                  pl.BlockSpec((None, 1, d), lambda i: ((i // per_b) * 6 + 5, 0, 0)),
                  pl.BlockSpec((1, d), lambda i: (0, 0)),
                  pl.BlockSpec((1, d), lambda i: (0, 0))],
        out_specs=pl.BlockSpec((tm, d), lambda i: (i, 0)),
        compiler_params=_cparams(("arbitrary",)),
        name="combine",
    )(y4, y4, y4, y4, w_t, x1, ada3, g2, b2)


def kernel(x, c, w_ada, b_ada, w_in, w_gla_gate_up, b_gla_gate, attn_sinks, gla_norm_gain,
           w_branch_att, w_branch_gla, w_out, ln1_gain, ln1_bias, w_router, b_router,
           w_gate_up, b_gate_up, w_down, b_down, ln2_gain, ln2_bias):
    bsz, seq, d = x.shape
    n = bsz * seq
    qa_w = ATT_HEADS * ATT_HEAD_DIM
    kv_w = ATT_KV_HEADS * ATT_HEAD_DIM
    gk_w = d // 2
    splits = (qa_w, kv_w, kv_w, gk_w, gk_w, d, d, GLA_GATE_RANK, d, d)
    offs = [0]
    for wdt in splits:
        offs.append(offs[-1] + wdt)
    ne = w_router.shape[-1]
    f = w_down.shape[-2]

    x2 = x.reshape(n, d)
    for l in range(w_in.shape[0]):
        wl = w_in[l]
        part = lambda i: wl[:, offs[i]:offs[i + 1]]
        w_main = jnp.concatenate([part(3), part(4), part(5), part(6), part(8), part(9),
                                  part(0), part(1), part(2)], axis=1).astype(BF16)
        w_alr = jnp.pad(part(7), ((0, 0), (0, LANES - GLA_GATE_RANK))).astype(BF16)
        wg_pad = jnp.pad(w_gla_gate_up[l], ((0, LANES - GLA_GATE_RANK), (0, 0)))
        wr_t = w_router[l].T
        wr_hi = wr_t.astype(BF16)
        wr_lo = (wr_t - wr_hi.astype(F32)).astype(BF16)
        bgu = b_gate_up[l].reshape(ne, 1, f, 2)

        ada = _ada(c, w_ada[l], b_ada[l])
        ada3 = ada.reshape(bsz * 6, 1, d)

        proj, alr = _proj(x2, ada3, w_main, w_alr, seq)
        y_att = _swa(proj, attn_sinks[l], bsz, seq,
                     q_blk=(2 * gk_w + 4 * d) // qa_w,
                     k_blk=(2 * gk_w + 4 * d + qa_w) // kv_w,
                     v_blk=(2 * gk_w + 4 * d + qa_w + kv_w) // kv_w)
        dk = gk_w // GLA_HEADS
        dv = d // GLA_HEADS
        y_gla = _gla(proj, alr, wg_pad, b_gla_gate[l].reshape(1, gk_w),
                     gla_norm_gain[l].reshape(1, d), bsz, seq,
                     q_blk=0, k_blk=gk_w // dk, v_blk=2 * gk_w // dv, r_blk=(2 * gk_w + d) // dv)
        x1, h2, logits_t = _merge(
            y_att, y_gla, proj, x2, ada3,
            w_branch_att[l].astype(BF16), w_branch_gla[l].astype(BF16), w_out[l].astype(BF16),
            ln1_gain[l].reshape(1, d), ln1_bias[l].reshape(1, d),
            wr_hi, wr_lo, b_router[l].reshape(ne, 1), seq,
            ga_blk=(2 * gk_w + 2 * d) // d, gg_blk=(2 * gk_w + 3 * d) // d)

        n_rows = n * TOP_K
        cap = -(-n_rows // MOE_BLOCK) * MOE_BLOCK + ne * MOE_BLOCK
        n_blocks = cap // MOE_BLOCK
        dest, w_top, blk_e, blk_end, valid_end = _route(logits_t, n_blocks)
        dest_flat = dest.reshape(-1)
        blk_e = blk_e.reshape(-1)[:n_blocks]
        blk_end = blk_end[:, 0]
        inv = _invert(dest_flat, valid_end[:, 0], blk_end, cap)
        y4 = _experts(blk_e, blk_end, inv, h2, w_gate_up[l], w_down[l],
                      bgu[..., 0], bgu[..., 1], b_down[l].reshape(ne, 1, d))
        x2 = _combine(y4, w_top.T, x1, ada3,
                      ln2_gain[l].reshape(1, d), ln2_bias[l].reshape(1, d), seq)
    return x2.reshape(bsz, seq, d)
```

```python
import jax
import jax.numpy as jnp
from jax import lax
from jax.experimental import pallas as pl
from jax.experimental.pallas import tpu as pltpu

F32 = jnp.float32
BF16 = jnp.bfloat16
I32 = jnp.int32

ATT_HEADS = 16
ATT_KV_HEADS = 2
ATT_HEAD_DIM = 64
WINDOW = 128
GLA_HEADS = 4
GLA_GATE_RANK = 16
GLA_GATE_TEMP = 16.0
GLA_CHUNK = 64
N_EXPERTS = 32
TOP_K = 4
SWIGLU_LIMIT = 7.0
SWIGLU_ALPHA = 1.702
LN_EPS = 1e-5
DEPTH = 1
DN_ALPHA = (2 * DEPTH) ** 0.25

LANES = 128
VMEM_LIMIT = 56 * 1024 * 1024

MOE_BLOCK = 256
PROJ_TM = 1024
PROJ_TN = 1152
GLA_STEP = 256
GLA_HEADS_PER_STEP = 4
MERGE_TM = 256
MERGE_SUBTILES = 2
ROW_TM = 256
ROW_UNROLL = 8
INVERT_STEPS = 32
EXPERT_TF = 1024
LN_ROWS = 256

NT_DIMS = (((1,), (1,)), ((), ()))
TN_DIMS = (((0,), (0,)), ((), ()))


def _cparams(sem, vmem=VMEM_LIMIT):
    return pltpu.CompilerParams(dimension_semantics=sem, vmem_limit_bytes=vmem)


def _ln(x):
    mu = jnp.mean(x, axis=-1, keepdims=True)
    xc = x - mu
    var = jnp.mean(xc * xc, axis=-1, keepdims=True)
    return xc * lax.rsqrt(var + LN_EPS)


def _sigmoid(x):
    return 1.0 / (1.0 + jnp.exp(-x))


def _ada_kernel(c_ref, w_ref, b_ref, o_ref):
    c = c_ref[...]
    s = c * _sigmoid(c)
    o_ref[...] = jnp.dot(s.astype(BF16), w_ref[...].astype(BF16),
                         preferred_element_type=F32) + b_ref[...]


def _ada(c, w_ada, b_ada):
    bsz, d = c.shape
    n = w_ada.shape[1]
    tn = 1536
    return pl.pallas_call(
        _ada_kernel,
        out_shape=jax.ShapeDtypeStruct((bsz, n), F32),
        grid=(n // tn,),
        in_specs=[pl.BlockSpec((bsz, d), lambda j: (0, 0)),
                  pl.BlockSpec((d, tn), lambda j: (0, j)),
                  pl.BlockSpec((1, tn), lambda j: (0, j))],
        out_specs=pl.BlockSpec((bsz, tn), lambda j: (0, j)),
        compiler_params=_cparams(("arbitrary",)),
        name="ada",
    )(c, w_ada, b_ada.reshape(1, n))


def _proj_kernel(x_ref, sc_ref, sh_ref, w_ref, walr_ref, o_ref, alr_ref, h_scr):
    @pl.when(pl.program_id(1) == 0)
    def _():
        tm = x_ref.shape[0]
        sub = min(LN_ROWS, tm)
        for r0 in range(0, tm, sub):
            rows = slice(r0, r0 + sub)
            h = _ln(x_ref[rows, :]) * (1.0 + sc_ref[...]) + sh_ref[...]
            h_scr[rows, :] = h.astype(BF16)
        alr_ref[...] = jnp.dot(h_scr[...], walr_ref[...], preferred_element_type=F32)

    o_ref[...] = jnp.dot(h_scr[...], w_ref[...],
                         preferred_element_type=F32).astype(BF16)


def _proj(x2, ada3, w_main, w_alr, seq):
    n, d = x2.shape
    cols = w_main.shape[1]
    tm, tn = min(PROJ_TM, seq), PROJ_TN
    per_b = seq // tm
    return pl.pallas_call(
        _proj_kernel,
        out_shape=(jax.ShapeDtypeStruct((n, cols), BF16),
                   jax.ShapeDtypeStruct((n, LANES), F32)),
        grid=(n // tm, cols // tn),
        in_specs=[pl.BlockSpec((tm, d), lambda i, j: (i, 0)),
                  pl.BlockSpec((None, 1, d), lambda i, j: ((i // per_b) * 6 + 1, 0, 0)),
                  pl.BlockSpec((None, 1, d), lambda i, j: ((i // per_b) * 6 + 0, 0, 0)),
                  pl.BlockSpec((d, tn), lambda i, j: (0, j)),
                  pl.BlockSpec((d, LANES), lambda i, j: (0, 0))],
        out_specs=(pl.BlockSpec((tm, tn), lambda i, j: (i, j)),
                   pl.BlockSpec((tm, LANES), lambda i, j: (i, 0))),
        scratch_shapes=[pltpu.VMEM((tm, d), BF16)],
        compiler_params=_cparams(("arbitrary", "arbitrary")),
        name="proj",
    )(x2, ada3, ada3, w_main, w_alr)


def _swa_kernel(sink_ref, q_ref, kp_ref, kc_ref, vp_ref, vc_ref, o_ref):
    n = pl.program_id(1)
    blk = WINDOW
    half = ATT_HEAD_DIM
    group = ATT_HEADS // ATT_KV_HEADS

    lane = lax.broadcasted_iota(I32, (2 * blk, 2 * half), 1)
    lo = lane < half

    def variants(prev_ref, cur_ref, scale):
        band = jnp.concatenate([prev_ref[...], cur_ref[...]], axis=0).astype(F32) * scale
        rolled = pltpu.roll(band, half, axis=1)
        zero = jnp.zeros_like(band)
        return ((jnp.where(lo, band, zero).astype(BF16), jnp.where(lo, zero, rolled).astype(BF16)),
                (jnp.where(lo, rolled, zero).astype(BF16), jnp.where(lo, zero, band).astype(BF16)))

    kvar = variants(kp_ref, kc_ref, ATT_HEAD_DIM ** -0.5)
    vvar = variants(vp_ref, vc_ref, 1.0)

    qi = lax.broadcasted_iota(I32, (blk, 2 * blk), 0)
    si = lax.broadcasted_iota(I32, (blk, 2 * blk), 1)
    dist = qi + blk - si
    first_key = jnp.where(n > 0, 0, blk)
    valid = (dist >= 0) & (dist < WINDOW) & (si >= first_key)
    dist_m = jnp.where(valid, dist.astype(F32), jnp.inf)

    for pair in range(ATT_HEADS // 2):
        q_pair = q_ref[:, pair * 2 * half:(pair + 1) * 2 * half]
        acc = jnp.zeros((blk, 2 * half), F32)
        for sub in range(2):
            h = pair * 2 + sub
            kh = h // group
            slope = 2.0 ** (-8.0 * (h + 1.0) / ATT_HEADS)
            s = lax.dot_general(q_pair, kvar[kh][sub], NT_DIMS, preferred_element_type=F32)
            s = s - slope * dist_m
            sink = sink_ref[h]
            m = jnp.maximum(jnp.max(s, axis=-1, keepdims=True), sink)
            p = jnp.exp(s - m)
            denom = jnp.sum(p, axis=-1, keepdims=True) + jnp.exp(sink - m)
            pv = jnp.dot(p.astype(BF16), vvar[kh][sub], preferred_element_type=F32)
            acc = acc + pv * (1.0 / denom)
        o_ref[:, pair * 2 * half:(pair + 1) * 2 * half] = acc.astype(BF16)


def _swa(proj, sinks, bsz, seq, q_blk, k_blk, v_blk):
    n = bsz * seq
    nb = seq // WINDOW
    qw = ATT_HEADS * ATT_HEAD_DIM
    kw = ATT_KV_HEADS * ATT_HEAD_DIM
    cur = lambda col: (lambda b, i: (b * nb + i, col))
    prev = lambda col: (lambda b, i: (b * nb + jnp.maximum(i - 1, 0), col))
    return pl.pallas_call(
        _swa_kernel,
        out_shape=jax.ShapeDtypeStruct((n, qw), BF16),
        grid=(bsz, nb),
        in_specs=[pl.BlockSpec(memory_space=pltpu.SMEM),
                  pl.BlockSpec((WINDOW, qw), cur(q_blk)),
                  pl.BlockSpec((WINDOW, kw), prev(k_blk)),
                  pl.BlockSpec((WINDOW, kw), cur(k_blk)),
                  pl.BlockSpec((WINDOW, kw), prev(v_blk)),
                  pl.BlockSpec((WINDOW, kw), cur(v_blk))],
        out_specs=pl.BlockSpec((WINDOW, qw), lambda b, i: (b * nb + i, 0)),
        compiler_params=_cparams(("arbitrary", "arbitrary")),
        name="swa",
    )(sinks, proj, proj, proj, proj, proj)


def _gla_kernel(q_ref, k_ref, v_ref, r_ref, alr_ref, wg_ref, bg_ref, gain_ref, o_ref, st_scr):
    c = GLA_CHUNK
    hp = st_scr.shape[0]
    ts = q_ref.shape[0]
    dk = q_ref.shape[1] // hp
    dv = v_ref.shape[1] // hp
    nc = ts // c

    @pl.when(pl.program_id(2) == 0)
    def _():
        st_scr[...] = jnp.zeros_like(st_scr)

    z = jnp.dot(alr_ref[...], wg_ref[...], preferred_element_type=F32,
                precision=lax.Precision.HIGHEST) + bg_ref[...]
    log_a = (jnp.minimum(z, 0.0) - jnp.log1p(jnp.exp(-jnp.abs(z)))) / GLA_GATE_TEMP
    row = lax.broadcasted_iota(I32, (ts, ts), 0)
    col = lax.broadcasted_iota(I32, (ts, ts), 1)
    causal = (row >= col) & ((row // c) == (col // c))
    b = jnp.dot(causal.astype(F32), log_a, preferred_element_type=F32,
                precision=lax.Precision.HIGHEST)
    b_last = [b[(ci + 1) * c - 1:(ci + 1) * c, :] for ci in range(nc)]
    b_end = jnp.concatenate([jnp.broadcast_to(bl, (c, hp * dk)) for bl in b_last], axis=0)
    q = q_ref[...].astype(F32)
    k = k_ref[...].astype(F32)
    q_e = (q * jnp.exp(b) * (dk ** -0.5)).astype(BF16)
    k_e = (k * jnp.exp(-b)).astype(BF16)
    k_end = (k * jnp.exp(b_end - b)).astype(BF16)
    decay = [jnp.exp(bl) for bl in b_last]

    outs = []
    for h in range(hp):
        kc = slice(h * dk, (h + 1) * dk)
        v = v_ref[:, h * dv:(h + 1) * dv]
        a = lax.dot_general(q_e[:, kc], k_e[:, kc], NT_DIMS, preferred_element_type=F32)
        a = jnp.where(causal, a, 0.0).astype(BF16)
        o_intra = jnp.dot(a, v, preferred_element_type=F32)
        st = st_scr[h]
        o_inter = []
        for ci in range(nc):
            rows = slice(ci * c, (ci + 1) * c)
            o_inter.append(lax.dot_general(q_e[rows, kc], st.astype(BF16), NT_DIMS,
                                           preferred_element_type=F32))
            upd = lax.dot_general(v[rows, :], k_end[rows, kc], TN_DIMS,
                                  preferred_element_type=F32)
            st = st * decay[ci][:, kc] + upd
        st_scr[h] = st
        o = o_intra + jnp.concatenate(o_inter, axis=0)
        outs.append(o * lax.rsqrt(jnp.mean(o * o, axis=-1, keepdims=True) + LN_EPS))

    o = jnp.concatenate(outs, axis=1) * gain_ref[...]
    r = r_ref[...].astype(F32)
    o_ref[...] = (o * (r * _sigmoid(r))).astype(BF16)


def _gla(proj, alr, wg_pad, bg, gain, bsz, seq, q_blk, k_blk, v_blk, r_blk):
    n = bsz * seq
    hp = GLA_HEADS_PER_STEP
    dk = wg_pad.shape[1] // GLA_HEADS
    dv = gain.shape[1] // GLA_HEADS
    ts = min(GLA_STEP, seq)
    steps = seq // ts
    assert q_blk % hp == 0 and k_blk % hp == 0 and v_blk % hp == 0 and r_blk % hp == 0
    rowmap = lambda col0: (lambda b, h, t: (b * steps + t, col0 // hp + h))
    return pl.pallas_call(
        _gla_kernel,
        out_shape=jax.ShapeDtypeStruct((n, GLA_HEADS * dv), BF16),
        grid=(bsz, GLA_HEADS // hp, steps),
        in_specs=[pl.BlockSpec((ts, hp * dk), rowmap(q_blk)),
                  pl.BlockSpec((ts, hp * dk), rowmap(k_blk)),
                  pl.BlockSpec((ts, hp * dv), rowmap(v_blk)),
                  pl.BlockSpec((ts, hp * dv), rowmap(r_blk)),
                  pl.BlockSpec((ts, LANES), lambda b, h, t: (b * steps + t, 0)),
                  pl.BlockSpec((LANES, hp * dk), lambda b, h, t: (0, h)),
                  pl.BlockSpec((1, hp * dk), lambda b, h, t: (0, h)),
                  pl.BlockSpec((1, hp * dv), lambda b, h, t: (0, h))],
        out_specs=pl.BlockSpec((ts, hp * dv), rowmap(0)),
        scratch_shapes=[pltpu.VMEM((hp, dv, dk), F32)],
        compiler_params=_cparams(("arbitrary", "arbitrary", "arbitrary")),
        name="gla",
    )(proj, proj, proj, proj, alr, wg_pad, bg, gain)


def _merge_kernel(ya_ref, yg_ref, ga_ref, gg_ref, x_ref, gt1_ref, sc2_ref, sh2_ref,
                  pa_ref, pg_ref, wo_ref, g1_ref, b1_ref, wrh_ref, wrl_ref, br_ref,
                  x1_ref, h2_ref, lt_ref):
    tm = x_ref.shape[0]
    sub = tm // MERGE_SUBTILES
    for s in range(MERGE_SUBTILES):
        rows = slice(s * sub, (s + 1) * sub)
        a = jnp.dot(ya_ref[rows, :], pa_ref[...], preferred_element_type=F32)
        g = jnp.dot(yg_ref[rows, :], pg_ref[...], preferred_element_type=F32)
        merged = (_sigmoid(ga_ref[rows, :].astype(F32)) * a
                  + _sigmoid(gg_ref[rows, :].astype(F32)) * g)
        mix = jnp.dot(merged.astype(BF16), wo_ref[...], preferred_element_type=F32)
        x1 = _ln(DN_ALPHA * x_ref[rows, :] + gt1_ref[...] * mix) * g1_ref[...] + b1_ref[...]
        x1_ref[rows, :] = x1
        h2 = _ln(x1) * (1.0 + sc2_ref[...]) + sh2_ref[...]
        h2_ref[rows, :] = h2
        h2_hi = h2.astype(BF16)
        h2_lo = (h2 - h2_hi.astype(F32)).astype(BF16)
        wrh = wrh_ref[...]
        logits = (lax.dot_general(wrh, h2_hi, NT_DIMS, preferred_element_type=F32)
                  + lax.dot_general(wrh, h2_lo, NT_DIMS, preferred_element_type=F32)
                  + lax.dot_general(wrl_ref[...], h2_hi, NT_DIMS, preferred_element_type=F32))
        lt_ref[:, rows] = logits + br_ref[...]


def _merge(y_att, y_gla, proj, x2, ada3, p_a, p_g, w_o, g1, b1, wr_hi, wr_lo, b_r, seq,
           ga_blk, gg_blk):
    n, d = x2.shape
    tm = min(MERGE_TM, seq)
    per_b = seq // tm
    ne = wr_hi.shape[0]
    const = lambda shape: pl.BlockSpec(shape, lambda i: (0,) * len(shape),
                                       pipeline_mode=pl.Buffered(1))
    adarow = lambda k: pl.BlockSpec((None, 1, d), lambda i: ((i // per_b) * 6 + k, 0, 0))
    return pl.pallas_call(
        _merge_kernel,
        out_shape=(jax.ShapeDtypeStruct((n, d), F32),
                   jax.ShapeDtypeStruct((n, d), F32),
                   jax.ShapeDtypeStruct((ne, n), F32)),
        grid=(n // tm,),
        in_specs=[pl.BlockSpec((tm, y_att.shape[1]), lambda i: (i, 0)),
                  pl.BlockSpec((tm, d), lambda i: (i, 0)),
                  pl.BlockSpec((tm, d), lambda i: (i, ga_blk)),
                  pl.BlockSpec((tm, d), lambda i: (i, gg_blk)),
                  pl.BlockSpec((tm, d), lambda i: (i, 0)),
                  adarow(2), adarow(4), adarow(3),
                  const(p_a.shape), const(p_g.shape), const(w_o.shape),
                  const((1, d)), const((1, d)),
                  const(wr_hi.shape), const(wr_lo.shape), const((ne, 1))],
        out_specs=(pl.BlockSpec((tm, d), lambda i: (i, 0)),
                   pl.BlockSpec((tm, d), lambda i: (i, 0)),
                   pl.BlockSpec((ne, tm), lambda i: (0, i))),
        compiler_params=_cparams(("arbitrary",)),
        name="merge",
    )(y_att, y_gla, proj, proj, x2, ada3, ada3, ada3, p_a, p_g, w_o, g1, b1, wr_hi, wr_lo, b_r)


def _route_kernel(lt_ref, dest_ref, w_ref, blke_ref, bend_ref, vend_ref, oh_scr, rank_scr):
    ne, n = lt_ref.shape
    logits = lt_ref[...]
    eidx = lax.broadcasted_iota(I32, (ne, n), 0).astype(F32)
    vals, idxs = [], []
    for _ in range(TOP_K):
        m = jnp.max(logits, axis=0, keepdims=True)
        idx = jnp.min(jnp.where(logits == m, eidx, float(ne)), axis=0, keepdims=True)
        vals.append(m)
        idxs.append(idx)
        logits = jnp.where(eidx == idx, -jnp.inf, logits)
    exps = [jnp.exp(v - vals[0]) for v in vals]
    total = exps[0] + exps[1] + exps[2] + exps[3]
    for k in range(TOP_K):
        w_ref[k:k + 1, :] = exps[k] / total

    onehot = jnp.zeros((ne, n), F32)
    for k in range(TOP_K):
        onehot = onehot + (eidx == idxs[k]).astype(F32)
    oh_scr[...] = onehot.astype(BF16)

    r = lax.broadcasted_iota(I32, (LANES, 2 * LANES), 0)
    cidx = lax.broadcasted_iota(I32, (LANES, 2 * LANES), 1)
    scan_mat = ((r < cidx) | (cidx >= LANES)).astype(BF16)
    carry = jnp.zeros((ne, LANES), F32)
    for t in range(n // LANES):
        cols = slice(t * LANES, (t + 1) * LANES)
        both = jnp.dot(oh_scr[:, cols], scan_mat, preferred_element_type=F32)
        rank_scr[:, cols] = both[:, :LANES] + carry
        carry = carry + both[:, LANES:]

    counts = carry
    nblk = jnp.floor((counts + (MOE_BLOCK - 1)) * (1.0 / MOE_BLOCK))
    er = lax.broadcasted_iota(I32, (ne, ne), 0)
    ec = lax.broadcasted_iota(I32, (ne, ne), 1)
    strict_lower = (ec < er).astype(BF16)
    blk_start = jnp.dot(strict_lower, nblk.astype(BF16), preferred_element_type=F32)
    blk_end = blk_start + nblk
    row_start = blk_start[:, :1] * float(MOE_BLOCK)

    pos = rank_scr[...] + row_start
    for k in range(TOP_K):
        d = jnp.sum(jnp.where(eidx == idxs[k], pos, 0.0), axis=0, keepdims=True)
        dest_ref[k:k + 1, :] = d.astype(I32)

    nb_pad = blke_ref.shape[1]
    j = lax.broadcasted_iota(I32, (ne, nb_pad), 1).astype(F32)
    blke = jnp.sum((blk_end[:, :1] <= j).astype(F32), axis=0, keepdims=True)
    blke_ref[...] = jnp.minimum(blke, float(ne - 1)).astype(I32)
    bend_ref[...] = blk_end.astype(I32)
    vend_ref[...] = (blk_start * float(MOE_BLOCK) + counts).astype(I32)


def _route(logits_t, n_blocks):
    ne, n = logits_t.shape
    nb_pad = -(-n_blocks // LANES) * LANES
    return pl.pallas_call(
        _route_kernel,
        out_shape=(jax.ShapeDtypeStruct((TOP_K, n), I32),
                   jax.ShapeDtypeStruct((TOP_K, n), F32),
                   jax.ShapeDtypeStruct((1, nb_pad), I32),
                   jax.ShapeDtypeStruct((ne, LANES), I32),
                   jax.ShapeDtypeStruct((ne, LANES), I32)),
        scratch_shapes=[pltpu.VMEM((ne, n), BF16), pltpu.VMEM((ne, n), F32)],
        compiler_params=pltpu.CompilerParams(vmem_limit_bytes=VMEM_LIMIT),
        name="route",
    )(logits_t)


def _invert_kernel(dest_ref, vend_ref, bend_ref, inv_ref):
    cap = inv_ref.shape[0]
    n_rows = dest_ref.shape[0]
    ne = bend_ref.shape[0]
    step = pl.program_id(0)
    scat_per = n_rows // (pl.num_programs(0) - 1)

    @pl.when(step == 0)
    def _():
        def pad(p, carry):
            inv_ref[p] = -1
            return carry
        for e in range(ne):
            lax.fori_loop(vend_ref[e], bend_ref[e] * MOE_BLOCK, pad, 0)
        lax.fori_loop(bend_ref[ne - 1] * MOE_BLOCK, cap, pad, 0)

    @pl.when(step >= 1)
    def _():
        def scatter(g, carry):
            b = (step - 1) * scat_per + g * ROW_UNROLL
            for u in range(ROW_UNROLL):
                inv_ref[dest_ref[b + u]] = b + u
            return carry
        lax.fori_loop(0, scat_per // ROW_UNROLL, scatter, 0)


def _invert(dest_flat, valid_end, blk_end, cap):
    steps = INVERT_STEPS
    assert dest_flat.shape[0] % (steps * ROW_UNROLL) == 0
    smem = pl.BlockSpec(memory_space=pltpu.SMEM)
    return pl.pallas_call(
        _invert_kernel,
        out_shape=jax.ShapeDtypeStruct((cap,), I32),
        grid=(1 + steps,),
        in_specs=[smem, smem, smem],
        out_specs=smem,
        compiler_params=pltpu.CompilerParams(dimension_semantics=("arbitrary",)),
        name="invert",
    )(dest_flat, valid_end, blk_end)


GU_CHUNK = 2 * LANES
EXPERT_RING = 5


def _expert_kernel(blke_ref, bend_ref, inv_ref, h_ref, wgu_ref, wd_ref, bg_ref, bu_ref, bd_ref,
                   y4_ref, wg_scr, wu_scr, wd_scr, gu_stage, wd_stage, xbuf, ybuf,
                   sems, gsem, ssem, fence):
    j = pl.program_id(0)
    n_blocks = pl.num_programs(0)
    ne = bend_ref.shape[0]
    n_used = bend_ref[ne - 1]
    e = blke_ref[j]
    e_end = bend_ref[e]
    d, f = wg_scr.shape
    bm = xbuf.shape[1]
    n_tok = h_ref.shape[0]
    spare0 = TOP_K * n_tok
    units = f // LANES
    ring = gu_stage.shape[0]
    first = (j == 0) | (e != blke_ref[jnp.maximum(j - 1, 0)])
    last = j == e_end - 1
    has_next = e_end < n_used
    nxt = blke_ref[jnp.minimum(e_end, n_blocks - 1)]
    slot = j % 2
    nslot = 1 - slot

    def gather_row(blk, dslot, r):
        tok = lax.rem(jnp.maximum(inv_ref[blk * bm + r], 0), n_tok)
        pltpu.make_async_copy(h_ref.at[pl.ds(tok, 1), :], xbuf.at[dslot, pl.ds(r, 1), :],
                              gsem.at[dslot]).start()

    def issue_gather(blk, dslot):
        for r in range(bm):
            gather_row(blk, dslot, r)

    def wait_gather(dslot):
        pltpu.make_async_copy(h_ref.at[pl.ds(0, bm), :], xbuf.at[dslot], gsem.at[dslot]).wait()

    def scatter_row(blk, sslot, all_spare, r):
        flat = inv_ref[blk * bm + r]
        row = jnp.where(all_spare | (flat < 0), spare0 + sslot * bm + r, flat)
        pltpu.make_async_copy(ybuf.at[sslot, pl.ds(r, 1), :], y4_ref.at[pl.ds(row, 1), :],
                              ssem.at[sslot]).start()

    def issue_scatter(blk, sslot, all_spare):
        for r in range(bm):
            scatter_row(blk, sslot, all_spare, r)

    def wait_scatter(sslot):
        pltpu.make_async_copy(ybuf.at[sslot], y4_ref.at[pl.ds(0, bm), :], ssem.at[sslot]).wait()

    def gu_copy(ex, u, wslot):
        return pltpu.make_async_copy(wgu_ref.at[ex, :, pl.ds(u * GU_CHUNK, GU_CHUNK)],
                                     gu_stage.at[wslot], sems.at[0, wslot])

    def wd_copy(ex, u, wslot):
        return pltpu.make_async_copy(wd_ref.at[ex, pl.ds(u * LANES, LANES), :],
                                     wd_stage.at[wslot], sems.at[1, wslot])

    def start_unit(ex, u):
        gu_copy(ex, u, u % ring).start()
        wd_copy(ex, u, u % ring).start()

    def convert_unit(ex, u):
        wslot = u % ring
        gu_copy(ex, u, wslot).wait()
        wd_copy(ex, u, wslot).wait()
        src = lax.broadcasted_iota(I32, (GU_CHUNK, GU_CHUNK), 0)
        dst = lax.broadcasted_iota(I32, (GU_CHUNK, GU_CHUNK), 1)
        perm = (dst == (src >> 1) + (src & 1) * LANES).astype(BF16)
        cols = slice(u * LANES, (u + 1) * LANES)
        sep = jnp.dot(gu_stage[wslot].astype(BF16), perm, preferred_element_type=F32)
        wg_scr[:, cols] = sep[:, :LANES].astype(BF16)
        wu_scr[:, cols] = sep[:, LANES:].astype(BF16)
        wd_scr[cols, :] = wd_stage[wslot].astype(BF16)
        if u + ring < units:
            start_unit(ex, u + ring)

    @pl.when(j == 0)
    def _():
        ybuf[...] = jnp.zeros_like(ybuf)
        pltpu.make_async_copy(ybuf.at[0], y4_ref.at[pl.ds(spare0, bm), :], ssem.at[0]).start()
        issue_gather(0, 0)
        for u in range(ring):
            start_unit(e, u)
        for u in range(units):
            convert_unit(e, u)

    @pl.when((j < n_used) & first & has_next)
    def _():
        for u in range(ring):
            start_unit(nxt, u)

    def compute(replace):
        wait_gather(slot)
        x = xbuf[slot].astype(BF16)
        prev_blk = jnp.maximum(j - 1, 0)
        next_blk = jnp.minimum(j + 1, n_blocks - 1)
        jobs = ([lambda r=r: gather_row(next_blk, nslot, r) for r in range(bm)]
                + [lambda r=r: scatter_row(prev_blk, nslot, j == 0, r) for r in range(bm)])
        tf = min(EXPERT_TF, f)
        n_chunks = f // tf
        after = tuple(sorted({n_chunks // 4, n_chunks // 2}))
        per_point = -(-len(jobs) // len(after))

        def row_jobs():
            for _ in range(min(per_point, len(jobs))):
                jobs.pop(0)()
            pl.semaphore_signal(fence, 1)
            pl.semaphore_wait(fence, 1)

        acc = jnp.zeros((bm, d), F32)
        for c in range(n_chunks):
            cols = slice(c * tf, (c + 1) * tf)
            g = jnp.dot(x, wg_scr[:, cols], preferred_element_type=F32) + bg_ref[:, cols]
            u = jnp.dot(x, wu_scr[:, cols], preferred_element_type=F32) + bu_ref[:, cols]
            gate = jnp.minimum(g, SWIGLU_LIMIT)
            up = jnp.clip(u, -SWIGLU_LIMIT, SWIGLU_LIMIT)
            act = (up + 1.0) * gate * _sigmoid(SWIGLU_ALPHA * gate)
            acc = acc + jnp.dot(act.astype(BF16), wd_scr[cols, :], preferred_element_type=F32)
            if c in after:
                row_jobs()
            if replace:
                for unit in range(c * tf // LANES, (c + 1) * tf // LANES):
                    convert_unit(nxt, unit)
        wait_scatter(slot)
        ybuf[slot] = acc + bd_ref[...]

    @pl.when((j < n_used) & last & has_next)
    def _():
        compute(True)

    @pl.when((j < n_used) & jnp.logical_not(last & has_next))
    def _():
        compute(False)

    @pl.when(j == n_used - 1)
    def _():
        issue_scatter(j, slot, False)
        wait_scatter(slot)
        wait_scatter(nslot)
        wait_gather(nslot)


def _experts(blk_e, blk_end, inv, h2, w_gate_up, w_down, b_gate, b_up, b_down):
    n, d = h2.shape
    ne, f, _ = w_down.shape
    n_blocks = blk_e.shape[0]
    ring = min(EXPERT_RING, f // LANES)
    bspec = lambda width: pl.BlockSpec((None, 1, width), lambda j, be, bend, iv: (be[j], 0, 0))
    hbm = pl.BlockSpec(memory_space=pl.ANY)
    return pl.pallas_call(
        _expert_kernel,
        out_shape=jax.ShapeDtypeStruct((TOP_K * n + 2 * MOE_BLOCK, d), F32),
        grid_spec=pltpu.PrefetchScalarGridSpec(
            num_scalar_prefetch=3,
            grid=(n_blocks,),
            in_specs=[hbm, hbm, hbm, bspec(f), bspec(f), bspec(d)],
            out_specs=hbm,
            scratch_shapes=[pltpu.VMEM((d, f), BF16), pltpu.VMEM((d, f), BF16),
                            pltpu.VMEM((f, d), BF16),
                            pltpu.VMEM((ring, d, GU_CHUNK), F32), pltpu.VMEM((ring, LANES, d), F32),
                            pltpu.VMEM((2, MOE_BLOCK, d), F32), pltpu.VMEM((2, MOE_BLOCK, d), F32),
                            pltpu.SemaphoreType.DMA((2, ring)),
                            pltpu.SemaphoreType.DMA((2,)), pltpu.SemaphoreType.DMA((2,)),
                            pltpu.SemaphoreType.REGULAR]),
        compiler_params=_cparams(("arbitrary",)),
        name="experts",
    )(blk_e, blk_end, inv, h2, w_gate_up, w_down, b_gate, b_up, b_down)


def _combine_kernel(y0_ref, y1_ref, y2_ref, y3_ref, wt_ref, x1_ref, gt2_ref, g2_ref, b2_ref, o_ref):
    wt = wt_ref[...]
    ffn = y0_ref[...] * wt[:, 0:1]
    for k, y_ref in enumerate((y1_ref, y2_ref, y3_ref), start=1):
        ffn = ffn + y_ref[...] * wt[:, k:k + 1]
    z = DN_ALPHA * x1_ref[...] + gt2_ref[...] * ffn
    o_ref[...] = _ln(z) * g2_ref[...] + b2_ref[...]


def _combine(y4, w_t, x1, ada3, g2, b2, seq):
    n, d = x1.shape
    tm = min(ROW_TM, seq)
    per_b = seq // tm
    tiles = n // tm
    yspec = lambda k: pl.BlockSpec((tm, d), lambda i: (k * tiles + i, 0))
    return pl.pallas_call(
        _combine_kernel,
        out_shape=jax.ShapeDtypeStruct((n, d), F32),
        grid=(tiles,),
        in_specs=[yspec(0), yspec(1), yspec(2), yspec(3),
                  pl.BlockSpec((tm, TOP_K), lambda i: (i, 0)),
                  pl.BlockSpec((tm, d), lambda i: (i, 0)),
                  pl.BlockSpec((None, 1, d), lambda i: ((i // per_b) * 6 + 5, 0, 0)),
                  pl.BlockSpec((1, d), lambda i: (0, 0)),
                  pl.BlockSpec((1, d), lambda i: (0, 0))],
        out_specs=pl.BlockSpec((tm, d), lambda i: (i, 0)),
        compiler_params=_cparams(("arbitrary",)),
        name="combine",
    )(y4, y4, y4, y4, w_t, x1, ada3, g2, b2)


def kernel(x, c, w_ada, b_ada, w_in, w_gla_gate_up, b_gla_gate, attn_sinks, gla_norm_gain,
           w_branch_att, w_branch_gla, w_out, ln1_gain, ln1_bias, w_router, b_router,
           w_gate_up, b_gate_up, w_down, b_down, ln2_gain, ln2_bias):
    bsz, seq, d = x.shape
    n = bsz * seq
    qa_w = ATT_HEADS * ATT_HEAD_DIM
    kv_w = ATT_KV_HEADS * ATT_HEAD_DIM
    gk_w = d // 2
    splits = (qa_w, kv_w, kv_w, gk_w, gk_w, d, d, GLA_GATE_RANK, d, d)
    offs = [0]
    for wdt in splits:
        offs.append(offs[-1] + wdt)
    ne = w_router.shape[-1]
    f = w_down.shape[-2]

    x2 = x.reshape(n, d)
    for l in range(w_in.shape[0]):
        wl = w_in[l]
        part = lambda i: wl[:, offs[i]:offs[i + 1]]
        w_main = jnp.concatenate([part(3), part(4), part(5), part(6), part(8), part(9),
                                  part(0), part(1), part(2)], axis=1).astype(BF16)
        w_alr = jnp.pad(part(7), ((0, 0), (0, LANES - GLA_GATE_RANK))).astype(BF16)
        wg_pad = jnp.pad(w_gla_gate_up[l], ((0, LANES - GLA_GATE_RANK), (0, 0)))
        wr_t = w_router[l].T
        wr_hi = wr_t.astype(BF16)
        wr_lo = (wr_t - wr_hi.astype(F32)).astype(BF16)
        bgu = b_gate_up[l].reshape(ne, 1, f, 2)

        ada = _ada(c, w_ada[l], b_ada[l])
        ada3 = ada.reshape(bsz * 6, 1, d)

        proj, alr = _proj(x2, ada3, w_main, w_alr, seq)
        y_att = _swa(proj, attn_sinks[l], bsz, seq,
                     q_blk=(2 * gk_w + 4 * d) // qa_w,
                     k_blk=(2 * gk_w + 4 * d + qa_w) // kv_w,
                     v_blk=(2 * gk_w + 4 * d + qa_w + kv_w) // kv_w)
        dk = gk_w // GLA_HEADS
        dv = d // GLA_HEADS
        y_gla = _gla(proj, alr, wg_pad, b_gla_gate[l].reshape(1, gk_w),
                     gla_norm_gain[l].reshape(1, d), bsz, seq,
                     q_blk=0, k_blk=gk_w // dk, v_blk=2 * gk_w // dv, r_blk=(2 * gk_w + d) // dv)
        x1, h2, logits_t = _merge(
            y_att, y_gla, proj, x2, ada3,
            w_branch_att[l].astype(BF16), w_branch_gla[l].astype(BF16), w_out[l].astype(BF16),
            ln1_gain[l].reshape(1, d), ln1_bias[l].reshape(1, d),
            wr_hi, wr_lo, b_router[l].reshape(ne, 1), seq,
            ga_blk=(2 * gk_w + 2 * d) // d, gg_blk=(2 * gk_w + 3 * d) // d)

        n_rows = n * TOP_K
        cap = -(-n_rows // MOE_BLOCK) * MOE_BLOCK + ne * MOE_BLOCK
        n_blocks = cap // MOE_BLOCK
        dest, w_top, blk_e, blk_end, valid_end = _route(logits_t, n_blocks)
        dest_flat = dest.reshape(-1)
        blk_e = blk_e.reshape(-1)[:n_blocks]
        blk_end = blk_end[:, 0]
        inv = _invert(dest_flat, valid_end[:, 0], blk_end, cap)
        y4 = _experts(blk_e, blk_end, inv, h2, w_gate_up[l], w_down[l],
                      bgu[..., 0], bgu[..., 1], b_down[l].reshape(ne, 1, d))
        x2 = _combine(y4, w_top.T, x1, ada3,
                      ln2_gain[l].reshape(1, d), ln2_bias[l].reshape(1, d), seq)
    return x2.reshape(bsz, seq, d)
```

```python
import jax
import jax.numpy as jnp
from jax import lax
from jax.experimental import pallas as pl
from jax.experimental.pallas import tpu as pltpu

F32 = jnp.float32
BF16 = jnp.bfloat16
I32 = jnp.int32

ATT_HEADS = 16
ATT_KV_HEADS = 2
ATT_HEAD_DIM = 64
WINDOW = 128
GLA_HEADS = 4
GLA_GATE_RANK = 16
GLA_GATE_TEMP = 16.0
GLA_CHUNK = 64
N_EXPERTS = 32
TOP_K = 4
SWIGLU_LIMIT = 7.0
SWIGLU_ALPHA = 1.702
LN_EPS = 1e-5
DEPTH = 1
DN_ALPHA = (2 * DEPTH) ** 0.25

LANES = 128
VMEM_LIMIT = 56 * 1024 * 1024

MOE_BLOCK = 256
PROJ_TM = 1024
PROJ_TN = 1152
GLA_STEP = 256
GLA_HEADS_PER_STEP = 4
MERGE_TM = 256
MERGE_SUBTILES = 2
ROW_TM = 256
ROW_UNROLL = 8
INVERT_STEPS = 32
EXPERT_TF = 1024
LN_ROWS = 256

NT_DIMS = (((1,), (1,)), ((), ()))
TN_DIMS = (((0,), (0,)), ((), ()))


def _cparams(sem, vmem=VMEM_LIMIT):
    return pltpu.CompilerParams(dimension_semantics=sem, vmem_limit_bytes=vmem)


def _ln(x):
    mu = jnp.mean(x, axis=-1, keepdims=True)
    xc = x - mu
    var = jnp.mean(xc * xc, axis=-1, keepdims=True)
    return xc * lax.rsqrt(var + LN_EPS)


def _sigmoid(x):
    return 1.0 / (1.0 + jnp.exp(-x))


def _ada_kernel(c_ref, w_ref, b_ref, o_ref):
    c = c_ref[...]
    s = c * _sigmoid(c)
    o_ref[...] = jnp.dot(s.astype(BF16), w_ref[...].astype(BF16),
                         preferred_element_type=F32) + b_ref[...]


def _ada(c, w_ada, b_ada):
    bsz, d = c.shape
    n = w_ada.shape[1]
    tn = 1536
    return pl.pallas_call(
        _ada_kernel,
        out_shape=jax.ShapeDtypeStruct((bsz, n), F32),
        grid=(n // tn,),
        in_specs=[pl.BlockSpec((bsz, d), lambda j: (0, 0)),
                  pl.BlockSpec((d, tn), lambda j: (0, j)),
                  pl.BlockSpec((1, tn), lambda j: (0, j))],
        out_specs=pl.BlockSpec((bsz, tn), lambda j: (0, j)),
        compiler_params=_cparams(("arbitrary",)),
        name="ada",
    )(c, w_ada, b_ada.reshape(1, n))


def _proj_kernel(x_ref, sc_ref, sh_ref, w_ref, walr_ref, o_ref, alr_ref, h_scr):
    @pl.when(pl.program_id(1) == 0)
    def _():
        tm = x_ref.shape[0]
        sub = min(LN_ROWS, tm)
        for r0 in range(0, tm, sub):
            rows = slice(r0, r0 + sub)
            h = _ln(x_ref[rows, :]) * (1.0 + sc_ref[...]) + sh_ref[...]
            h_scr[rows, :] = h.astype(BF16)
        alr_ref[...] = jnp.dot(h_scr[...], walr_ref[...], preferred_element_type=F32)

    o_ref[...] = jnp.dot(h_scr[...], w_ref[...],
                         preferred_element_type=F32).astype(BF16)


def _proj(x2, ada3, w_main, w_alr, seq):
    n, d = x2.shape
    cols = w_main.shape[1]
    tm, tn = min(PROJ_TM, seq), PROJ_TN
    per_b = seq // tm
    return pl.pallas_call(
        _proj_kernel,
        out_shape=(jax.ShapeDtypeStruct((n, cols), BF16),
                   jax.ShapeDtypeStruct((n, LANES), F32)),
        grid=(n // tm, cols // tn),
        in_specs=[pl.BlockSpec((tm, d), lambda i, j: (i, 0)),
                  pl.BlockSpec((None, 1, d), lambda i, j: ((i // per_b) * 6 + 1, 0, 0)),
                  pl.BlockSpec((None, 1, d), lambda i, j: ((i // per_b) * 6 + 0, 0, 0)),
                  pl.BlockSpec((d, tn), lambda i, j: (0, j)),
                  pl.BlockSpec((d, LANES), lambda i, j: (0, 0))],
        out_specs=(pl.BlockSpec((tm, tn), lambda i, j: (i, j)),
                   pl.BlockSpec((tm, LANES), lambda i, j: (i, 0))),
        scratch_shapes=[pltpu.VMEM((tm, d), BF16)],
        compiler_params=_cparams(("arbitrary", "arbitrary")),
        name="proj",
    )(x2, ada3, ada3, w_main, w_alr)


def _swa_kernel(sink_ref, q_ref, kp_ref, kc_ref, vp_ref, vc_ref, o_ref):
    n = pl.program_id(1)
    blk = WINDOW
    half = ATT_HEAD_DIM
    group = ATT_HEADS // ATT_KV_HEADS

    lane = lax.broadcasted_iota(I32, (2 * blk, 2 * half), 1)
    lo = lane < half

    def variants(prev_ref, cur_ref, scale):
        band = jnp.concatenate([prev_ref[...], cur_ref[...]], axis=0).astype(F32) * scale
        rolled = pltpu.roll(band, half, axis=1)
        zero = jnp.zeros_like(band)
        return ((jnp.where(lo, band, zero).astype(BF16), jnp.where(lo, zero, rolled).astype(BF16)),
                (jnp.where(lo, rolled, zero).astype(BF16), jnp.where(lo, zero, band).astype(BF16)))

    kvar = variants(kp_ref, kc_ref, ATT_HEAD_DIM ** -0.5)
    vvar = variants(vp_ref, vc_ref, 1.0)

    qi = lax.broadcasted_iota(I32, (blk, 2 * blk), 0)
    si = lax.broadcasted_iota(I32, (blk, 2 * blk), 1)
    dist = qi + blk - si
    first_key = jnp.where(n > 0, 0, blk)
    valid = (dist >= 0) & (dist < WINDOW) & (si >= first_key)
    dist_m = jnp.where(valid, dist.astype(F32), jnp.inf)

    for pair in range(ATT_HEADS // 2):
        q_pair = q_ref[:, pair * 2 * half:(pair + 1) * 2 * half]
        acc = jnp.zeros((blk, 2 * half), F32)
        for sub in range(2):
            h = pair * 2 + sub
            kh = h // group
            slope = 2.0 ** (-8.0 * (h + 1.0) / ATT_HEADS)
            s = lax.dot_general(q_pair, kvar[kh][sub], NT_DIMS, preferred_element_type=F32)
            s = s - slope * dist_m
            sink = sink_ref[h]
            m = jnp.maximum(jnp.max(s, axis=-1, keepdims=True), sink)
            p = jnp.exp(s - m)
            denom = jnp.sum(p, axis=-1, keepdims=True) + jnp.exp(sink - m)
            pv = jnp.dot(p.astype(BF16), vvar[kh][sub], preferred_element_type=F32)
            acc = acc + pv * (1.0 / denom)
        o_ref[:, pair * 2 * half:(pair + 1) * 2 * half] = acc.astype(BF16)


def _swa(proj, sinks, bsz, seq, q_blk, k_blk, v_blk):
    n = bsz * seq
    nb = seq // WINDOW
    qw = ATT_HEADS * ATT_HEAD_DIM
    kw = ATT_KV_HEADS * ATT_HEAD_DIM
    cur = lambda col: (lambda b, i: (b * nb + i, col))
    prev = lambda col: (lambda b, i: (b * nb + jnp.maximum(i - 1, 0), col))
    return pl.pallas_call(
        _swa_kernel,
        out_shape=jax.ShapeDtypeStruct((n, qw), BF16),
        grid=(bsz, nb),
        in_specs=[pl.BlockSpec(memory_space=pltpu.SMEM),
                  pl.BlockSpec((WINDOW, qw), cur(q_blk)),
                  pl.BlockSpec((WINDOW, kw), prev(k_blk)),
                  pl.BlockSpec((WINDOW, kw), cur(k_blk)),
                  pl.BlockSpec((WINDOW, kw), prev(v_blk)),
                  pl.BlockSpec((WINDOW, kw), cur(v_blk))],
        out_specs=pl.BlockSpec((WINDOW, qw), lambda b, i: (b * nb + i, 0)),
        compiler_params=_cparams(("arbitrary", "arbitrary")),
        name="swa",
    )(sinks, proj, proj, proj, proj, proj)


def _gla_kernel(q_ref, k_ref, v_ref, r_ref, alr_ref, wg_ref, bg_ref, gain_ref, o_ref, st_scr):
    c = GLA_CHUNK
    hp = st_scr.shape[0]
    ts = q_ref.shape[0]
    dk = q_ref.shape[1] // hp
    dv = v_ref.shape[1] // hp
    nc = ts // c

    @pl.when(pl.program_id(2) == 0)
    def _():
        st_scr[...] = jnp.zeros_like(st_scr)

    z = jnp.dot(alr_ref[...], wg_ref[...], preferred_element_type=F32,
                precision=lax.Precision.HIGHEST) + bg_ref[...]
    log_a = (jnp.minimum(z, 0.0) - jnp.log1p(jnp.exp(-jnp.abs(z)))) / GLA_GATE_TEMP
    row = lax.broadcasted_iota(I32, (ts, ts), 0)
    col = lax.broadcasted_iota(I32, (ts, ts), 1)
    causal = (row >= col) & ((row // c) == (col // c))
    b = jnp.dot(causal.astype(F32), log_a, preferred_element_type=F32,
                precision=lax.Precision.HIGHEST)
    b_last = [b[(ci + 1) * c - 1:(ci + 1) * c, :] for ci in range(nc)]
    b_end = jnp.concatenate([jnp.broadcast_to(bl, (c, hp * dk)) for bl in b_last], axis=0)
    q = q_ref[...].astype(F32)
    k = k_ref[...].astype(F32)
    q_e = (q * jnp.exp(b) * (dk ** -0.5)).astype(BF16)
    k_e = (k * jnp.exp(-b)).astype(BF16)
    k_end = (k * jnp.exp(b_end - b)).astype(BF16)
    decay = [jnp.exp(bl) for bl in b_last]

    outs = []
    for h in range(hp):
        kc = slice(h * dk, (h + 1) * dk)
        v = v_ref[:, h * dv:(h + 1) * dv]
        a = lax.dot_general(q_e[:, kc], k_e[:, kc], NT_DIMS, preferred_element_type=F32)
        a = jnp.where(causal, a, 0.0).astype(BF16)
        o_intra = jnp.dot(a, v, preferred_element_type=F32)
        st = st_scr[h]
        o_inter = []
        for ci in range(nc):
            rows = slice(ci * c, (ci + 1) * c)
            o_inter.append(lax.dot_general(q_e[rows, kc], st.astype(BF16), NT_DIMS,
                                           preferred_element_type=F32))
            upd = lax.dot_general(v[rows, :], k_end[rows, kc], TN_DIMS,
                                  preferred_element_type=F32)
            st = st * decay[ci][:, kc] + upd
        st_scr[h] = st
        o = o_intra + jnp.concatenate(o_inter, axis=0)
        outs.append(o * lax.rsqrt(jnp.mean(o * o, axis=-1, keepdims=True) + LN_EPS))

    o = jnp.concatenate(outs, axis=1) * gain_ref[...]
    r = r_ref[...].astype(F32)
    o_ref[...] = (o * (r * _sigmoid(r))).astype(BF16)


def _gla(proj, alr, wg_pad, bg, gain, bsz, seq, q_blk, k_blk, v_blk, r_blk):
    n = bsz * seq
    hp = GLA_HEADS_PER_STEP
    dk = wg_pad.shape[1] // GLA_HEADS
    dv = gain.shape[1] // GLA_HEADS
    ts = min(GLA_STEP, seq)
    steps = seq // ts
    assert q_blk % hp == 0 and k_blk % hp == 0 and v_blk % hp == 0 and r_blk % hp == 0
    rowmap = lambda col0: (lambda b, h, t: (b * steps + t, col0 // hp + h))
    return pl.pallas_call(
        _gla_kernel,
        out_shape=jax.ShapeDtypeStruct((n, GLA_HEADS * dv), BF16),
        grid=(bsz, GLA_HEADS // hp, steps),
        in_specs=[pl.BlockSpec((ts, hp * dk), rowmap(q_blk)),
                  pl.BlockSpec((ts, hp * dk), rowmap(k_blk)),
                  pl.BlockSpec((ts, hp * dv), rowmap(v_blk)),
                  pl.BlockSpec((ts, hp * dv), rowmap(r_blk)),
                  pl.BlockSpec((ts, LANES), lambda b, h, t: (b * steps + t, 0)),
                  pl.BlockSpec((LANES, hp * dk), lambda b, h, t: (0, h)),
                  pl.BlockSpec((1, hp * dk), lambda b, h, t: (0, h)),
                  pl.BlockSpec((1, hp * dv), lambda b, h, t: (0, h))],
        out_specs=pl.BlockSpec((ts, hp * dv), rowmap(0)),
        scratch_shapes=[pltpu.VMEM((hp, dv, dk), F32)],
        compiler_params=_cparams(("arbitrary", "arbitrary", "arbitrary")),
        name="gla",
    )(proj, proj, proj, proj, alr, wg_pad, bg, gain)


def _merge_kernel(ya_ref, yg_ref, ga_ref, gg_ref, x_ref, gt1_ref, sc2_ref, sh2_ref,
                  pa_ref, pg_ref, wo_ref, g1_ref, b1_ref, wrh_ref, wrl_ref, br_ref,
                  x1_ref, h2_ref, lt_ref):
    tm = x_ref.shape[0]
    sub = tm // MERGE_SUBTILES
    for s in range(MERGE_SUBTILES):
        rows = slice(s * sub, (s + 1) * sub)
        a = jnp.dot(ya_ref[rows, :], pa_ref[...], preferred_element_type=F32)
        g = jnp.dot(yg_ref[rows, :], pg_ref[...], preferred_element_type=F32)
        merged = (_sigmoid(ga_ref[rows, :].astype(F32)) * a
                  + _sigmoid(gg_ref[rows, :].astype(F32)) * g)
        mix = jnp.dot(merged.astype(BF16), wo_ref[...], preferred_element_type=F32)
        x1 = _ln(DN_ALPHA * x_ref[rows, :] + gt1_ref[...] * mix) * g1_ref[...] + b1_ref[...]
        x1_ref[rows, :] = x1
        h2 = _ln(x1) * (1.0 + sc2_ref[...]) + sh2_ref[...]
        h2_ref[rows, :] = h2
        h2_hi = h2.astype(BF16)
        h2_lo = (h2 - h2_hi.astype(F32)).astype(BF16)
        wrh = wrh_ref[...]
        logits = (lax.dot_general(wrh, h2_hi, NT_DIMS, preferred_element_type=F32)
                  + lax.dot_general(wrh, h2_lo, NT_DIMS, preferred_element_type=F32)
                  + lax.dot_general(wrl_ref[...], h2_hi, NT_DIMS, preferred_element_type=F32))
        lt_ref[:, rows] = logits + br_ref[...]


def _merge(y_att, y_gla, proj, x2, ada3, p_a, p_g, w_o, g1, b1, wr_hi, wr_lo, b_r, seq,
           ga_blk, gg_blk):
    n, d = x2.shape
    tm = min(MERGE_TM, seq)
    per_b = seq // tm
    ne = wr_hi.shape[0]
    const = lambda shape: pl.BlockSpec(shape, lambda i: (0,) * len(shape),
                                       pipeline_mode=pl.Buffered(1))
    adarow = lambda k: pl.BlockSpec((None, 1, d), lambda i: ((i // per_b) * 6 + k, 0, 0))
    return pl.pallas_call(
        _merge_kernel,
        out_shape=(jax.ShapeDtypeStruct((n, d), F32),
                   jax.ShapeDtypeStruct((n, d), F32),
                   jax.ShapeDtypeStruct((ne, n), F32)),
        grid=(n // tm,),
        in_specs=[pl.BlockSpec((tm, y_att.shape[1]), lambda i: (i, 0)),
                  pl.BlockSpec((tm, d), lambda i: (i, 0)),
                  pl.BlockSpec((tm, d), lambda i: (i, ga_blk)),
                  pl.BlockSpec((tm, d), lambda i: (i, gg_blk)),
                  pl.BlockSpec((tm, d), lambda i: (i, 0)),
                  adarow(2), adarow(4), adarow(3),
                  const(p_a.shape), const(p_g.shape), const(w_o.shape),
                  const((1, d)), const((1, d)),
                  const(wr_hi.shape), const(wr_lo.shape), const((ne, 1))],
        out_specs=(pl.BlockSpec((tm, d), lambda i: (i, 0)),
                   pl.BlockSpec((tm, d), lambda i: (i, 0)),
                   pl.BlockSpec((ne, tm), lambda i: (0, i))),
        compiler_params=_cparams(("arbitrary",)),
        name="merge",
    )(y_att, y_gla, proj, proj, x2, ada3, ada3, ada3, p_a, p_g, w_o, g1, b1, wr_hi, wr_lo, b_r)


def _route_kernel(lt_ref, dest_ref, w_ref, blke_ref, bend_ref, vend_ref, oh_scr, rank_scr):
    ne, n = lt_ref.shape
    logits = lt_ref[...]
    eidx = lax.broadcasted_iota(I32, (ne, n), 0).astype(F32)
    vals, idxs = [], []
    for _ in range(TOP_K):
        m = jnp.max(logits, axis=0, keepdims=True)
        idx = jnp.min(jnp.where(logits == m, eidx, float(ne)), axis=0, keepdims=True)
        vals.append(m)
        idxs.append(idx)
        logits = jnp.where(eidx == idx, -jnp.inf, logits)
    exps = [jnp.exp(v - vals[0]) for v in vals]
    total = exps[0] + exps[1] + exps[2] + exps[3]
    for k in range(TOP_K):
        w_ref[k:k + 1, :] = exps[k] / total

    onehot = jnp.zeros((ne, n), F32)
    for k in range(TOP_K):
        onehot = onehot + (eidx == idxs[k]).astype(F32)
    oh_scr[...] = onehot.astype(BF16)

    r = lax.broadcasted_iota(I32, (LANES, 2 * LANES), 0)
    cidx = lax.broadcasted_iota(I32, (LANES, 2 * LANES), 1)
    scan_mat = ((r < cidx) | (cidx >= LANES)).astype(BF16)
    carry = jnp.zeros((ne, LANES), F32)
    for t in range(n // LANES):
        cols = slice(t * LANES, (t + 1) * LANES)
        both = jnp.dot(oh_scr[:, cols], scan_mat, preferred_element_type=F32)
        rank_scr[:, cols] = both[:, :LANES] + carry
        carry = carry + both[:, LANES:]

    counts = carry
    nblk = jnp.floor((counts + (MOE_BLOCK - 1)) * (1.0 / MOE_BLOCK))
    er = lax.broadcasted_iota(I32, (ne, ne), 0)
    ec = lax.broadcasted_iota(I32, (ne, ne), 1)
    strict_lower = (ec < er).astype(BF16)
    blk_start = jnp.dot(strict_lower, nblk.astype(BF16), preferred_element_type=F32)
    blk_end = blk_start + nblk
    row_start = blk_start[:, :1] * float(MOE_BLOCK)

    pos = rank_scr[...] + row_start
    for k in range(TOP_K):
        d = jnp.sum(jnp.where(eidx == idxs[k], pos, 0.0), axis=0, keepdims=True)
        dest_ref[k:k + 1, :] = d.astype(I32)

    nb_pad = blke_ref.shape[1]
    j = lax.broadcasted_iota(I32, (ne, nb_pad), 1).astype(F32)
    blke = jnp.sum((blk_end[:, :1] <= j).astype(F32), axis=0, keepdims=True)
    blke_ref[...] = jnp.minimum(blke, float(ne - 1)).astype(I32)
    bend_ref[...] = blk_end.astype(I32)
    vend_ref[...] = (blk_start * float(MOE_BLOCK) + counts).astype(I32)


def _route(logits_t, n_blocks):
    ne, n = logits_t.shape
    nb_pad = -(-n_blocks // LANES) * LANES
    return pl.pallas_call(
        _route_kernel,
        out_shape=(jax.ShapeDtypeStruct((TOP_K, n), I32),
                   jax.ShapeDtypeStruct((TOP_K, n), F32),
                   jax.ShapeDtypeStruct((1, nb_pad), I32),
                   jax.ShapeDtypeStruct((ne, LANES), I32),
                   jax.ShapeDtypeStruct((ne, LANES), I32)),
        scratch_shapes=[pltpu.VMEM((ne, n), BF16), pltpu.VMEM((ne, n), F32)],
        compiler_params=pltpu.CompilerParams(vmem_limit_bytes=VMEM_LIMIT),
        name="route",
    )(logits_t)


def _invert_kernel(dest_ref, vend_ref, bend_ref, inv_ref):
    cap = inv_ref.shape[0]
    n_rows = dest_ref.shape[0]
    ne = bend_ref.shape[0]
    step = pl.program_id(0)
    scat_per = n_rows // (pl.num_programs(0) - 1)

    @pl.when(step == 0)
    def _():
        def pad(p, carry):
            inv_ref[p] = -1
            return carry
        for e in range(ne):
            lax.fori_loop(vend_ref[e], bend_ref[e] * MOE_BLOCK, pad, 0)
        lax.fori_loop(bend_ref[ne - 1] * MOE_BLOCK, cap, pad, 0)

    @pl.when(step >= 1)
    def _():
        def scatter(g, carry):
            b = (step - 1) * scat_per + g * ROW_UNROLL
            for u in range(ROW_UNROLL):
                inv_ref[dest_ref[b + u]] = b + u
            return carry
        lax.fori_loop(0, scat_per // ROW_UNROLL, scatter, 0)


def _invert(dest_flat, valid_end, blk_end, cap):
    steps = INVERT_STEPS
    assert dest_flat.shape[0] % (steps * ROW_UNROLL) == 0
    smem = pl.BlockSpec(memory_space=pltpu.SMEM)
    return pl.pallas_call(
        _invert_kernel,
        out_shape=jax.ShapeDtypeStruct((cap,), I32),
        grid=(1 + steps,),
        in_specs=[smem, smem, smem],
        out_specs=smem,
        compiler_params=pltpu.CompilerParams(dimension_semantics=("arbitrary",)),
        name="invert",
    )(dest_flat, valid_end, blk_end)


GU_CHUNK = 2 * LANES
EXPERT_RING = 5


def _expert_kernel(blke_ref, bend_ref, inv_ref, h_ref, wgu_ref, wd_ref, bg_ref, bu_ref, bd_ref,
                   y4_ref, wg_scr, wu_scr, wd_scr, gu_stage, wd_stage, xbuf, ybuf,
                   sems, gsem, ssem, fence):
    j = pl.program_id(0)
    n_blocks = pl.num_programs(0)
    ne = bend_ref.shape[0]
    n_used = bend_ref[ne - 1]
    e = blke_ref[j]
    e_end = bend_ref[e]
    d, f = wg_scr.shape
    bm = xbuf.shape[1]
    n_tok = h_ref.shape[0]
    spare0 = TOP_K * n_tok
    units = f // LANES
    ring = gu_stage.shape[0]
    first = (j == 0) | (e != blke_ref[jnp.maximum(j - 1, 0)])
    last = j == e_end - 1
    has_next = e_end < n_used
    nxt = blke_ref[jnp.minimum(e_end, n_blocks - 1)]
    slot = j % 2
    nslot = 1 - slot

    def gather_row(blk, dslot, r):
        tok = lax.rem(jnp.maximum(inv_ref[blk * bm + r], 0), n_tok)
        pltpu.make_async_copy(h_ref.at[pl.ds(tok, 1), :], xbuf.at[dslot, pl.ds(r, 1), :],
                              gsem.at[dslot]).start(priority=r % 2)

    def issue_gather(blk, dslot):
        for r in range(bm):
            gather_row(blk, dslot, r)

    def wait_gather(dslot):
        pltpu.make_async_copy(h_ref.at[pl.ds(0, bm), :], xbuf.at[dslot], gsem.at[dslot]).wait()

    def scatter_row(blk, sslot, all_spare, r):
        flat = inv_ref[blk * bm + r]
        row = jnp.where(all_spare | (flat < 0), spare0 + sslot * bm + r, flat)
        pltpu.make_async_copy(ybuf.at[sslot, pl.ds(r, 1), :], y4_ref.at[pl.ds(row, 1), :],
                              ssem.at[sslot]).start(priority=r % 2)

    def issue_scatter(blk, sslot, all_spare):
        for r in range(bm):
            scatter_row(blk, sslot, all_spare, r)

    def wait_scatter(sslot):
        pltpu.make_async_copy(ybuf.at[sslot], y4_ref.at[pl.ds(0, bm), :], ssem.at[sslot]).wait()

    def gu_copy(ex, u, wslot):
        return pltpu.make_async_copy(wgu_ref.at[ex, :, pl.ds(u * GU_CHUNK, GU_CHUNK)],
                                     gu_stage.at[wslot], sems.at[0, wslot])

    def wd_copy(ex, u, wslot):
        return pltpu.make_async_copy(wd_ref.at[ex, pl.ds(u * LANES, LANES), :],
                                     wd_stage.at[wslot], sems.at[1, wslot])

    def start_unit(ex, u):
        gu_copy(ex, u, u % ring).start()
        wd_copy(ex, u, u % ring).start()

    def convert_unit(ex, u):
        wslot = u % ring
        gu_copy(ex, u, wslot).wait()
        wd_copy(ex, u, wslot).wait()
        src = lax.broadcasted_iota(I32, (GU_CHUNK, GU_CHUNK), 0)
        dst = lax.broadcasted_iota(I32, (GU_CHUNK, GU_CHUNK), 1)
        perm = (dst == (src >> 1) + (src & 1) * LANES).astype(BF16)
        cols = slice(u * LANES, (u + 1) * LANES)
        sep = jnp.dot(gu_stage[wslot].astype(BF16), perm, preferred_element_type=F32)
        wg_scr[:, cols] = sep[:, :LANES].astype(BF16)
        wu_scr[:, cols] = sep[:, LANES:].astype(BF16)
        wd_scr[cols, :] = wd_stage[wslot].astype(BF16)
        if u + ring < units:
            start_unit(ex, u + ring)

    @pl.when(j == 0)
    def _():
        ybuf[...] = jnp.zeros_like(ybuf)
        pltpu.make_async_copy(ybuf.at[0], y4_ref.at[pl.ds(spare0, bm), :], ssem.at[0]).start()
        issue_gather(0, 0)
        for u in range(ring):
            start_unit(e, u)
        for u in range(units):
            convert_unit(e, u)

    @pl.when((j < n_used) & first & has_next)
    def _():
        for u in range(ring):
            start_unit(nxt, u)

    def compute(replace):
        wait_gather(slot)
        x = xbuf[slot].astype(BF16)
        prev_blk = jnp.maximum(j - 1, 0)
        next_blk = jnp.minimum(j + 1, n_blocks - 1)
        jobs = ([lambda r=r: gather_row(next_blk, nslot, r) for r in range(bm)]
                + [lambda r=r: scatter_row(prev_blk, nslot, j == 0, r) for r in range(bm)])
        tf = min(EXPERT_TF, f)
        n_chunks = f // tf
        after = tuple(sorted({n_chunks // 4, n_chunks // 2}))
        per_point = -(-len(jobs) // len(after))

        def row_jobs():
            for _ in range(min(per_point, len(jobs))):
                jobs.pop(0)()
            pl.semaphore_signal(fence, 1)
            pl.semaphore_wait(fence, 1)

        acc = jnp.zeros((bm, d), F32)
        for c in range(n_chunks):
            cols = slice(c * tf, (c + 1) * tf)
            g = jnp.dot(x, wg_scr[:, cols], preferred_element_type=F32) + bg_ref[:, cols]
            u = jnp.dot(x, wu_scr[:, cols], preferred_element_type=F32) + bu_ref[:, cols]
            gate = jnp.minimum(g, SWIGLU_LIMIT)
            up = jnp.clip(u, -SWIGLU_LIMIT, SWIGLU_LIMIT)
            act = (up + 1.0) * gate * _sigmoid(SWIGLU_ALPHA * gate)
            acc = acc + jnp.dot(act.astype(BF16), wd_scr[cols, :], preferred_element_type=F32)
            if c in after:
                row_jobs()
            if replace:
                for unit in range(c * tf // LANES, (c + 1) * tf // LANES):
                    convert_unit(nxt, unit)
        wait_scatter(slot)
        ybuf[slot] = acc + bd_ref[...]

    @pl.when((j < n_used) & last & has_next)
    def _():
        compute(True)

    @pl.when((j < n_used) & jnp.logical_not(last & has_next))
    def _():
        compute(False)

    @pl.when(j == n_used - 1)
    def _():
        issue_scatter(j, slot, False)
        wait_scatter(slot)
        wait_scatter(nslot)
        wait_gather(nslot)


def _experts(blk_e, blk_end, inv, h2, w_gate_up, w_down, b_gate, b_up, b_down):
    n, d = h2.shape
    ne, f, _ = w_down.shape
    n_blocks = blk_e.shape[0]
    ring = min(EXPERT_RING, f // LANES)
    bspec = lambda width: pl.BlockSpec((None, 1, width), lambda j, be, bend, iv: (be[j], 0, 0))
    hbm = pl.BlockSpec(memory_space=pl.ANY)
    return pl.pallas_call(
        _expert_kernel,
        out_shape=jax.ShapeDtypeStruct((TOP_K * n + 2 * MOE_BLOCK, d), F32),
        grid_spec=pltpu.PrefetchScalarGridSpec(
            num_scalar_prefetch=3,
            grid=(n_blocks,),
            in_specs=[hbm, hbm, hbm, bspec(f), bspec(f), bspec(d)],
            out_specs=hbm,
            scratch_shapes=[pltpu.VMEM((d, f), BF16), pltpu.VMEM((d, f), BF16),
                            pltpu.VMEM((f, d), BF16),
                            pltpu.VMEM((ring, d, GU_CHUNK), F32), pltpu.VMEM((ring, LANES, d), F32),
                            pltpu.VMEM((2, MOE_BLOCK, d), F32), pltpu.VMEM((2, MOE_BLOCK, d), F32),
                            pltpu.SemaphoreType.DMA((2, ring)),
                            pltpu.SemaphoreType.DMA((2,)), pltpu.SemaphoreType.DMA((2,)),
                            pltpu.SemaphoreType.REGULAR]),
        compiler_params=_cparams(("arbitrary",)),
        name="experts",
    )(blk_e, blk_end, inv, h2, w_gate_up, w_down, b_gate, b_up, b_down)


def _combine_kernel(y0_ref, y1_ref, y2_ref, y3_ref, wt_ref, x1_ref, gt2_ref, g2_ref, b2_ref, o_ref):
    wt = wt_ref[...]
    ffn = y0_ref[...] * wt[:, 0:1]
    for k, y_ref in enumerate((y1_ref, y2_ref, y3_ref), start=1):
        ffn = ffn + y_ref[...] * wt[:, k:k + 1]
    z = DN_ALPHA * x1_ref[...] + gt2_ref[...] * ffn
    o_ref[...] = _ln(z) * g2_ref[...] + b2_ref[...]


def _combine(y4, w_t, x1, ada3, g2, b2, seq):
    n, d = x1.shape
    tm = min(ROW_TM, seq)
    per_b = seq // tm
    tiles = n // tm
    yspec = lambda k: pl.BlockSpec((tm, d), lambda i: (k * tiles + i, 0))
    return pl.pallas_call(
        _combine_kernel,
        out_shape=jax.ShapeDtypeStruct((n, d), F32),
        grid=(tiles,),
        in_specs=[yspec(0), yspec(1), yspec(2), yspec(3),
                  pl.BlockSpec((tm, TOP_K), lambda i: (i, 0)),
                  pl.BlockSpec((tm, d), lambda i: (i, 0)),
                  pl.BlockSpec((None, 1, d), lambda i: ((i // per_b) * 6 + 5, 0, 0)),
                  pl.BlockSpec((1, d), lambda i: (0, 0)),
                  pl.BlockSpec((1, d), lambda i: (0, 0))],
        out_specs=pl.BlockSpec((tm, d), lambda i: (i, 0)),
        compiler_params=_cparams(("arbitrary",)),
        name="combine",
    )(y4, y4, y4, y4, w_t, x1, ada3, g2, b2)


def kernel(x, c, w_ada, b_ada, w_in, w_gla_gate_up, b_gla_gate, attn_sinks, gla_norm_gain,
           w_branch_att, w_branch_gla, w_out, ln1_gain, ln1_bias, w_router, b_router,
           w_gate_up, b_gate_up, w_down, b_down, ln2_gain, ln2_bias):
    bsz, seq, d = x.shape
    n = bsz * seq
    qa_w = ATT_HEADS * ATT_HEAD_DIM
    kv_w = ATT_KV_HEADS * ATT_HEAD_DIM
    gk_w = d // 2
    splits = (qa_w, kv_w, kv_w, gk_w, gk_w, d, d, GLA_GATE_RANK, d, d)
    offs = [0]
    for wdt in splits:
        offs.append(offs[-1] + wdt)
    ne = w_router.shape[-1]
    f = w_down.shape[-2]

    x2 = x.reshape(n, d)
    for l in range(w_in.shape[0]):
        wl = w_in[l]
        part = lambda i: wl[:, offs[i]:offs[i + 1]]
        w_main = jnp.concatenate([part(3), part(4), part(5), part(6), part(8), part(9),
                                  part(0), part(1), part(2)], axis=1).astype(BF16)
        w_alr = jnp.pad(part(7), ((0, 0), (0, LANES - GLA_GATE_RANK))).astype(BF16)
        wg_pad = jnp.pad(w_gla_gate_up[l], ((0, LANES - GLA_GATE_RANK), (0, 0)))
        wr_t = w_router[l].T
        wr_hi = wr_t.astype(BF16)
        wr_lo = (wr_t - wr_hi.astype(F32)).astype(BF16)
        bgu = b_gate_up[l].reshape(ne, 1, f, 2)

        ada = _ada(c, w_ada[l], b_ada[l])
        ada3 = ada.reshape(bsz * 6, 1, d)

        proj, alr = _proj(x2, ada3, w_main, w_alr, seq)
        y_att = _swa(proj, attn_sinks[l], bsz, seq,
                     q_blk=(2 * gk_w + 4 * d) // qa_w,
                     k_blk=(2 * gk_w + 4 * d + qa_w) // kv_w,
                     v_blk=(2 * gk_w + 4 * d + qa_w + kv_w) // kv_w)
        dk = gk_w // GLA_HEADS
        dv = d // GLA_HEADS
        y_gla = _gla(proj, alr, wg_pad, b_gla_gate[l].reshape(1, gk_w),
                     gla_norm_gain[l].reshape(1, d), bsz, seq,
                     q_blk=0, k_blk=gk_w // dk, v_blk=2 * gk_w // dv, r_blk=(2 * gk_w + d) // dv)
        x1, h2, logits_t = _merge(
            y_att, y_gla, proj, x2, ada3,
            w_branch_att[l].astype(BF16), w_branch_gla[l].astype(BF16), w_out[l].astype(BF16),
            ln1_gain[l].reshape(1, d), ln1_bias[l].reshape(1, d),
            wr_hi, wr_lo, b_router[l].reshape(ne, 1), seq,
            ga_blk=(2 * gk_w + 2 * d) // d, gg_blk=(2 * gk_w + 3 * d) // d)

        n_rows = n * TOP_K
        cap = -(-n_rows // MOE_BLOCK) * MOE_BLOCK + ne * MOE_BLOCK
        n_blocks = cap // MOE_BLOCK
        dest, w_top, blk_e, blk_end, valid_end = _route(logits_t, n_blocks)
        dest_flat = dest.reshape(-1)
        blk_e = blk_e.reshape(-1)[:n_blocks]
        blk_end = blk_end[:, 0]
        inv = _invert(dest_flat, valid_end[:, 0], blk_end, cap)
        y4 = _experts(blk_e, blk_end, inv, h2, w_gate_up[l], w_down[l],
                      bgu[..., 0], bgu[..., 1], b_down[l].reshape(ne, 1, d))
        x2 = _combine(y4, w_top.T, x1, ada3,
                      ln2_gain[l].reshape(1, d), ln2_bias[l].reshape(1, d), seq)
    return x2.reshape(bsz, seq, d)
```
